```python
import math
import jax, jax.numpy as jnp
from jax import lax
import numpy as np

D_MODEL = 2048
BATCH = 1
SEQ = 8192
DEPTH = 2
DEC_BATCH = 32
DEC_SEQ = 1
PAST_LEN = 8192
PAGE_SIZE = 128

N_A_LAYERS = DEPTH // 2
N_B_LAYERS = DEPTH - N_A_LAYERS
CHUNK = 128
D_GATE = D_MODEL
N_GROUPS_A = 16
GROUP_DIM_A = D_GATE // N_GROUPS_A
N_HEADS_B = 16
HEAD_DIM_B = D_MODEL // (2 * N_HEADS_B)
V_DIM_B = 2 * HEAD_DIM_B
D_FF = 4 * D_MODEL
Q_BLOCK = 128
RMS_EPS = 1e-5
LN_EPS = 1e-5

kernel_name = 'yoco_gmlp_diffattn_step'


def rmsnorm(x, g, eps=RMS_EPS):
    xf = x.astype(jnp.float32)
    y = xf * lax.rsqrt(jnp.mean(xf * xf, axis=-1, keepdims=True) + eps)
    return (y * g.astype(jnp.float32)).astype(x.dtype)


def layernorm(x, g, b, eps=LN_EPS):
    xf = x.astype(jnp.float32)
    mu = jnp.mean(xf, axis=-1, keepdims=True)
    var = jnp.mean(jnp.square(xf - mu), axis=-1, keepdims=True)
    y = (xf - mu) * lax.rsqrt(var + eps)
    return (y * g.astype(jnp.float32) + b.astype(jnp.float32)).astype(x.dtype)


def chunk_gating_mixer(x, norm_g, w_in, b_in, ln_g, ln_b, w_s, b_s, w_out):
    bsz, s, _ = x.shape
    h = rmsnorm(x, norm_g)
    z = jax.nn.gelu(h @ w_in + b_in, approximate=False)
    u, v = jnp.split(z, 2, axis=-1)
    v = layernorm(v, ln_g, ln_b)
    n_chunks = -(-s // CHUNK)
    pad = n_chunks * CHUNK - s
    vp = jnp.pad(v, ((0, 0), (0, pad), (0, 0))).reshape(bsz, n_chunks, CHUNK, N_GROUPS_A, GROUP_DIM_A)
    causal = jnp.tril(jnp.ones((CHUNK, CHUNK), dtype=bool))
    w_masked = jnp.where(causal[None], w_s, 0.0).astype(vp.dtype)
    mixed = jnp.einsum('gts,bcsgd->bctgd', w_masked, vp)
    mixed = mixed + jnp.swapaxes(b_s, 0, 1)[None, None, :, :, None]
    mixed = mixed.reshape(bsz, n_chunks * CHUNK, D_GATE)[:, :s]
    return (u * mixed) @ w_out, v


def sq_relu_mlp(x, g, w1, w2):
    h = rmsnorm(x, g)
    return jnp.square(jax.nn.relu(h @ w1)) @ w2


def shared_kv(h, g, w_k, w_v):
    bsz, s, _ = h.shape
    n = rmsnorm(h, g)
    k = (n @ w_k).reshape(bsz, s, 2 * N_HEADS_B, HEAD_DIM_B)
    v = (n @ w_v).reshape(bsz, s, N_HEADS_B, V_DIM_B)
    return k, v


def lambda_init(layer):
    return 0.8 - 0.6 * math.exp(-0.3 * layer)


def diff_lambda(lq1, lk1, lq2, lk2, lam_init):
    f = lambda a: a.astype(jnp.float32)
    return jnp.exp(jnp.sum(f(lq1) * f(lk1))) - jnp.exp(jnp.sum(f(lq2) * f(lk2))) + lam_init


def diff_weights(s, lam):
    p = jax.nn.softmax(s, axis=-1)
    p = p.reshape(p.shape[:-3] + (N_HEADS_B, 2) + p.shape[-2:])
    return p[..., 0, :, :] - lam * p[..., 1, :, :]


def diff_attn_prompt(q, k, v, lam):
    bsz, s = q.shape[0], q.shape[1]
    nqb = s // Q_BLOCK
    qb = jnp.swapaxes(q.reshape(bsz, nqb, Q_BLOCK, 2 * N_HEADS_B, HEAD_DIM_B), 0, 1)
    key_pos = jnp.arange(s)

    def block(args):
        qi, i = args
        sc = jnp.einsum('bqhd,bkhd->bhqk', qi, k).astype(jnp.float32)
        q_pos = i * Q_BLOCK + jnp.arange(Q_BLOCK)
        sc = jnp.where(key_pos[None, :] <= q_pos[:, None], sc, -jnp.inf)
        a = diff_weights(sc, lam).astype(v.dtype)
        return jnp.einsum('bhqk,bkhe->bqhe', a, v)

    o = lax.map(block, (qb, jnp.arange(nqb)))
    return jnp.swapaxes(o, 0, 1).reshape(bsz, s, N_HEADS_B, V_DIM_B)


def diff_attn_sample(q, k_new, v_new, cache_k, cache_v, page_table, lam):
    s_new = q.shape[1]
    causal = jnp.tril(jnp.ones((s_new, s_new), dtype=bool))

    def one_seq(args):
        qi, kn, vn, pt = args
        kp = cache_k[pt].reshape(-1, 2 * N_HEADS_B, HEAD_DIM_B).astype(qi.dtype)
        vp = cache_v[pt].reshape(-1, N_HEADS_B, V_DIM_B).astype(vn.dtype)
        n_past = kp.shape[0]
        sc_past = jnp.einsum('qhd,khd->hqk', qi, kp).astype(jnp.float32)
        sc_new = jnp.einsum('qhd,khd->hqk', qi, kn).astype(jnp.float32)
        sc_new = jnp.where(causal, sc_new, -jnp.inf)
        a = diff_weights(jnp.concatenate([sc_past, sc_new], axis=-1), lam)
        a_past = a[..., :n_past].astype(vp.dtype)
        a_new = a[..., n_past:].astype(vn.dtype)
        return jnp.einsum('hqk,khe->qhe', a_past, vp) + jnp.einsum('hqk,khe->qhe', a_new, vn)

    return lax.map(one_seq, (q, k_new, v_new, page_table))


def diff_out(o, subln_g, w_o, lam_init):
    o = rmsnorm(o, subln_g) * (1.0 - lam_init)
    return o.reshape(o.shape[0], o.shape[1], N_HEADS_B * V_DIM_B) @ w_o


def setup_inputs(seed: int = 0) -> dict:
    key = jax.random.key(seed)
    ks = iter(jax.random.split(key, 40))

    def nrm(shape, scale):
        return jax.random.normal(next(ks), shape, jnp.float32) * scale

    def gain(shape):
        return 1.0 + nrm(shape, 0.01)

    n_pages = PAST_LEN // PAGE_SIZE
    n_used = DEC_BATCH * n_pages
    n_phys = n_used + (n_used + 3) // 4
    perm = jax.random.permutation(next(ks), n_phys)
    page_table = perm[:n_used].reshape(DEC_BATCH, n_pages).astype(jnp.int32)

    return {
        'x_prompt': nrm((BATCH, SEQ, D_MODEL), 1.0),
        'x_sample': nrm((DEC_BATCH, DEC_SEQ, D_MODEL), 1.0),
        'cache_k': nrm((n_phys, PAGE_SIZE, 2 * N_HEADS_B, HEAD_DIM_B), 1.0),
        'cache_v': nrm((n_phys, PAGE_SIZE, N_HEADS_B, V_DIM_B), 1.0),
        'page_table': page_table,
        'a_norm_g': gain((N_A_LAYERS, D_MODEL)),
        'a_w_in': nrm((N_A_LAYERS, D_MODEL, 2 * D_GATE), D_MODEL ** -0.5),
        'a_b_in': nrm((N_A_LAYERS, 2 * D_GATE), 0.02),
        'a_ln_g': gain((N_A_LAYERS, D_GATE)),
        'a_ln_b': nrm((N_A_LAYERS, D_GATE), 0.02),
        'a_w_s': nrm((N_A_LAYERS, N_GROUPS_A, CHUNK, CHUNK), CHUNK ** -0.5),
        'a_b_s': gain((N_A_LAYERS, N_GROUPS_A, CHUNK)),
        'a_w_out': nrm((N_A_LAYERS, D_GATE, D_MODEL), D_GATE ** -0.5),
        'm_norm_g': gain((DEPTH, D_MODEL)),
        'm_w1': nrm((DEPTH, D_MODEL, D_FF), D_MODEL ** -0.5),
        'm_w2': nrm((DEPTH, D_FF, D_MODEL), D_FF ** -0.5),
        'kv_norm_g': gain((D_MODEL,)),
        'w_k': nrm((D_MODEL, 2 * N_HEADS_B * HEAD_DIM_B), D_MODEL ** -0.5),
        'w_v': nrm((D_MODEL, N_HEADS_B * V_DIM_B), D_MODEL ** -0.5),
        'b_norm_g': gain((N_B_LAYERS, D_MODEL)),
        'b_w_q': nrm((N_B_LAYERS, D_MODEL, 2 * N_HEADS_B * HEAD_DIM_B), D_MODEL ** -0.5),
        'b_lq1': nrm((N_B_LAYERS, HEAD_DIM_B), 0.1),
        'b_lk1': nrm((N_B_LAYERS, HEAD_DIM_B), 0.1),
        'b_lq2': nrm((N_B_LAYERS, HEAD_DIM_B), 0.1),
        'b_lk2': nrm((N_B_LAYERS, HEAD_DIM_B), 0.1),
        'b_subln_g': gain((N_B_LAYERS, V_DIM_B)),
        'b_w_o': nrm((N_B_LAYERS, N_HEADS_B * V_DIM_B, D_MODEL), (N_HEADS_B * V_DIM_B) ** -0.5),
        'final_norm_g': gain((D_MODEL,)),
    }


def reference(x_prompt, x_sample, cache_k, cache_v, page_table,
              a_norm_g, a_w_in, a_b_in, a_ln_g, a_ln_b, a_w_s, a_b_s, a_w_out,
              m_norm_g, m_w1, m_w2, kv_norm_g, w_k, w_v,
              b_norm_g, b_w_q, b_lq1, b_lk1, b_lq2, b_lk2, b_subln_g, b_w_o,
              final_norm_g):
    hp, hs = x_prompt, x_sample
    scale = HEAD_DIM_B ** -0.5
    chunk_v_rows = []
    k_prompt = v_prompt = k_sample = v_sample = None
    for layer in range(DEPTH):
        if layer < N_A_LAYERS:
            i = layer
            a_args = (a_norm_g[i], a_w_in[i], a_b_in[i], a_ln_g[i], a_ln_b[i], a_w_s[i], a_b_s[i], a_w_out[i])
            op, _ = chunk_gating_mixer(hp, *a_args)
            os_, v_rows = chunk_gating_mixer(hs, *a_args)
            hp = hp + op
            hs = hs + os_
            chunk_v_rows.append(v_rows)
        else:
            if layer == N_A_LAYERS:
                k_prompt, v_prompt = shared_kv(hp, kv_norm_g, w_k, w_v)
                k_sample, v_sample = shared_kv(hs, kv_norm_g, w_k, w_v)
            j = layer - N_A_LAYERS
            lam_init = lambda_init(layer)
            lam = diff_lambda(b_lq1[j], b_lk1[j], b_lq2[j], b_lk2[j], lam_init)
            qp = (rmsnorm(hp, b_norm_g[j]) @ b_w_q[j]).reshape(hp.shape[0], hp.shape[1], 2 * N_HEADS_B, HEAD_DIM_B) * scale
            qs = (rmsnorm(hs, b_norm_g[j]) @ b_w_q[j]).reshape(hs.shape[0], hs.shape[1], 2 * N_HEADS_B, HEAD_DIM_B) * scale
            op = diff_attn_prompt(qp, k_prompt, v_prompt, lam)
            os_ = diff_attn_sample(qs, k_sample, v_sample, cache_k, cache_v, page_table, lam)
            hp = hp + diff_out(op, b_subln_g[j], b_w_o[j], lam_init)
            hs = hs + diff_out(os_, b_subln_g[j], b_w_o[j], lam_init)
        hp = hp + sq_relu_mlp(hp, m_norm_g[layer], m_w1[layer], m_w2[layer])
        hs = hs + sq_relu_mlp(hs, m_norm_g[layer], m_w1[layer], m_w2[layer])
    y_prompt = rmsnorm(hp, final_norm_g)
    y_sample = rmsnorm(hs, final_norm_g)
    chunk_v_sample = jnp.stack(chunk_v_rows, axis=0)
    return (y_prompt, y_sample, k_prompt, v_prompt, k_sample, v_sample, chunk_v_sample)
```

```python
import functools
import math

import jax
import jax.numpy as jnp
import numpy as np
from jax import lax
from jax.experimental import pallas as pl
from jax.experimental.pallas import tpu as pltpu

F32 = jnp.float32
BF16 = jnp.bfloat16

CHUNK = 128
N_GROUPS_A = 16
N_HEADS_B = 16
RMS_EPS = 1e-5
LN_EPS = 1e-5
SQRT_HALF = math.sqrt(0.5)

V7X_LANES = 128
V7X_VMEM_LIMIT_BYTES = 56 * 1024 * 1024


def _lambda_init(layer):
    return 0.8 - 0.6 * math.exp(-0.3 * layer)


def _rms_scale(x):
    return x * lax.rsqrt(jnp.mean(x * x, axis=-1, keepdims=True) + RMS_EPS)


def _gelu_exact(z):
    return 0.5 * z * (1.0 + lax.erf(z * SQRT_HALF))


def _resident(shape):
    return pl.BlockSpec(shape, lambda *_: (0,) * len(shape), pipeline_mode=pl.Buffered(1))


def _params(*semantics):
    return pltpu.CompilerParams(dimension_semantics=semantics,
                                vmem_limit_bytes=V7X_VMEM_LIMIT_BYTES)


def _mixer_front(x, g_ref, win_ref, bin_ref, lng_ref, lnb_ref):
    d_gate = lng_ref.shape[-1]
    h = (_rms_scale(x) * g_ref[...]).astype(BF16)
    z = jnp.dot(h, win_ref[...], preferred_element_type=F32) + bin_ref[...]
    z = _gelu_exact(z)
    u = z[:, :d_gate]
    v = z[:, d_gate:]
    mu = jnp.mean(v, axis=-1, keepdims=True)
    vc = v - mu
    var = jnp.mean(vc * vc, axis=-1, keepdims=True)
    vn = vc * lax.rsqrt(var + LN_EPS) * lng_ref[...] + lnb_ref[...]
    return u, vn


def _mixer_prompt_kernel(x_ref, g_ref, win_ref, bin_ref, lng_ref, lnb_ref, ws_ref, bias_ref,
                         wout_ref, o_ref):
    x = x_ref[...]
    u, vn = _mixer_front(x, g_ref, win_ref, bin_ref, lng_ref, lnb_ref)
    vb = vn.astype(BF16)
    n_chunks = x.shape[0] // CHUNK
    gd = vn.shape[1] // N_GROUPS_A
    t_pos = lax.broadcasted_iota(jnp.int32, (CHUNK, CHUNK), 0)
    s_pos = lax.broadcasted_iota(jnp.int32, (CHUNK, CHUNK), 1)
    causal = s_pos <= t_pos
    blocks = [[None] * N_GROUPS_A for _ in range(n_chunks)]
    for g in range(N_GROUPS_A):
        w_masked = jnp.where(causal, ws_ref[g], 0.0).astype(BF16)
        rhs = jnp.concatenate(
            [vb[c * CHUNK:(c + 1) * CHUNK, g * gd:(g + 1) * gd] for c in range(n_chunks)], axis=1)
        mg = jnp.dot(w_masked, rhs, preferred_element_type=F32)
        for c in range(n_chunks):
            blocks[c][g] = mg[:, c * gd:(c + 1) * gd]
    bias = bias_ref[...]
    mixed = jnp.concatenate(
        [jnp.concatenate(blocks[c], axis=1) + bias for c in range(n_chunks)], axis=0)
    t = (u * mixed).astype(BF16)
    o_ref[...] = x + jnp.dot(t, wout_ref[...], preferred_element_type=F32)


def _mixer_sample_kernel(x_ref, g_ref, win_ref, bin_ref, lng_ref, lnb_ref, scale_ref, bias_ref,
                         wout_ref, o_ref, vn_ref):
    x = x_ref[...]
    u, vn = _mixer_front(x, g_ref, win_ref, bin_ref, lng_ref, lnb_ref)
    vn_ref[...] = vn
    mixed = vn * scale_ref[...] + bias_ref[...]
    t = (u * mixed).astype(BF16)
    o_ref[...] = x + jnp.dot(t, wout_ref[...], preferred_element_type=F32)


def _mixer_prompt(x, norm_g, w_in, b_in, ln_g, ln_b, w_s, bias_full, w_out, *, tm):
    m, d = x.shape
    dg = ln_g.shape[-1]
    row = lambda i: (i, 0)
    return pl.pallas_call(
        _mixer_prompt_kernel,
        grid=(m // tm,),
        in_specs=[pl.BlockSpec((tm, d), row), _resident((1, d)), _resident((d, 2 * dg)),
                  _resident((1, 2 * dg)), _resident((1, dg)), _resident((1, dg)),
                  _resident(w_s.shape), _resident((CHUNK, dg)), _resident((dg, d))],
        out_specs=pl.BlockSpec((tm, d), row),
        out_shape=jax.ShapeDtypeStruct((m, d), F32),
        compiler_params=_params("parallel"),
        name="mixer_prompt",
    )(x, norm_g, w_in, b_in, ln_g, ln_b, w_s, bias_full, w_out)


def _mixer_sample(x, norm_g, w_in, b_in, ln_g, ln_b, scale_row, bias_row, w_out):
    m, d = x.shape
    dg = ln_g.shape[-1]
    return pl.pallas_call(
        _mixer_sample_kernel,
        grid=(1,),
        in_specs=[_resident((m, d)), _resident((1, d)), _resident((d, 2 * dg)),
                  _resident((1, 2 * dg)), _resident((1, dg)), _resident((1, dg)),
                  _resident((1, dg)), _resident((1, dg)), _resident((dg, d))],
        out_specs=[pl.BlockSpec((m, d), lambda i: (0, 0)), pl.BlockSpec((m, dg), lambda i: (0, 0))],
        out_shape=[jax.ShapeDtypeStruct((m, d), F32), jax.ShapeDtypeStruct((m, dg), F32)],
        compiler_params=_params("arbitrary"),
        name="mixer_sample",
    )(x, norm_g, w_in, b_in, ln_g, ln_b, scale_row, bias_row, w_out)


def _mlp_kernel(x_ref, g_ref, w1_ref, w2_ref, *rest, final_norm):
    if final_norm:
        fg_ref, o_ref, hn_ref, acc_ref = rest
    else:
        o_ref, hn_ref, acc_ref = rest
    j = pl.program_id(1)

    @pl.when(j == 0)
    def _():
        x = x_ref[...]
        hn_ref[...] = (_rms_scale(x) * g_ref[...]).astype(BF16)
        acc_ref[...] = x

    a = jnp.dot(hn_ref[...], w1_ref[...], preferred_element_type=F32)
    a = jnp.maximum(a, 0.0)
    a = (a * a).astype(BF16)
    acc_ref[...] += jnp.dot(a, w2_ref[...], preferred_element_type=F32)

    @pl.when(j == pl.num_programs(1) - 1)
    def _():
        y = acc_ref[...]
        if final_norm:
            y = _rms_scale(y) * fg_ref[...]
        o_ref[...] = y


def _mlp(x, g, w1, w2, final_g=None, *, tm, tf):
    m, d = x.shape
    f = w1.shape[1]
    final_norm = final_g is not None
    in_specs = [pl.BlockSpec((tm, d), lambda i, j: (i, 0)), _resident((1, d)),
                pl.BlockSpec((d, tf), lambda i, j: (0, j)), pl.BlockSpec((tf, d), lambda i, j: (j, 0))]
    args = [x, g, w1, w2]
    if final_norm:
        in_specs.append(_resident((1, d)))
        args.append(final_g)
    return pl.pallas_call(
        functools.partial(_mlp_kernel, final_norm=final_norm),
        grid=(m // tm, f // tf),
        in_specs=in_specs,
        out_specs=pl.BlockSpec((tm, d), lambda i, j: (i, 0)),
        out_shape=jax.ShapeDtypeStruct((m, d), F32),
        scratch_shapes=[pltpu.VMEM((tm, d), BF16), pltpu.VMEM((tm, d), F32)],
        compiler_params=_params("parallel", "arbitrary"),
        name="mlp_final" if final_norm else "mlp",
    )(*args)


def _kvq_kernel(x_ref, gk_ref, gq_ref, wk_ref, wv_ref, wq_ref, k_ref, v_ref, kb_ref, vb_ref, qb_ref,
                *, q_scale):
    xs = _rms_scale(x_ref[...])
    nk = (xs * gk_ref[...]).astype(BF16)
    nq = (xs * gq_ref[...]).astype(BF16)
    k = jnp.dot(nk, wk_ref[...], preferred_element_type=F32)
    v = jnp.dot(nk, wv_ref[...], preferred_element_type=F32)
    q = jnp.dot(nq, wq_ref[...], preferred_element_type=F32) * q_scale
    k_ref[...] = k
    v_ref[...] = v
    kb_ref[...] = k.astype(BF16)
    vb_ref[...] = v.astype(BF16)
    qb_ref[...] = q.astype(BF16)


def _kvq(x, gk, gq, wk, wv, wq, *, tm, q_scale):
    m, d = x.shape
    row = lambda i: (i, 0)
    blk = lambda: pl.BlockSpec((tm, d), row)
    return pl.pallas_call(
        functools.partial(_kvq_kernel, q_scale=q_scale),
        grid=(m // tm,),
        in_specs=[blk(), _resident((1, d)), _resident((1, d)),
                  _resident(wk.shape), _resident(wv.shape), _resident(wq.shape)],
        out_specs=[blk(), blk(), blk(), blk(), blk()],
        out_shape=[jax.ShapeDtypeStruct((m, d), F32), jax.ShapeDtypeStruct((m, d), F32),
                   jax.ShapeDtypeStruct((m, d), BF16), jax.ShapeDtypeStruct((m, d), BF16),
                   jax.ShapeDtypeStruct((m, d), BF16)],
        compiler_params=_params("parallel"),
        name="kvq_proj",
    )(x, gk, gq, wk, wv, wq)


def _diff_subln(o1, o2, lam, g, out_gain):
    o = o1 - lam * o2
    return _rms_scale(o) * g * out_gain


def _attn_prompt_kernel(qi_ref, kj_ref, lam_ref, q_ref, k_ref, v_ref, g_ref, o_ref,
                        m1_ref, l1_ref, a1_ref, m2_ref, l2_ref, a2_ref, *, out_gain):
    t = pl.program_id(1)
    qi = qi_ref[t]
    kj = kj_ref[t]
    tq, dh2 = q_ref.shape
    tk = k_ref.shape[0]
    dh = dh2 // 2

    @pl.when(kj == 0)
    def _():
        for m_ref, l_ref, a_ref in ((m1_ref, l1_ref, a1_ref), (m2_ref, l2_ref, a2_ref)):
            m_ref[...] = jnp.full(m_ref.shape, -jnp.inf, F32)
            l_ref[...] = jnp.zeros(l_ref.shape, F32)
            a_ref[...] = jnp.zeros(a_ref.shape, F32)

    def step(masked):
        q = q_ref[...]
        k = k_ref[...]
        lane = lax.broadcasted_iota(jnp.int32, (1, dh2), 1)
        v_aug = jnp.concatenate([v_ref[...], jnp.ones((tk, dh2), BF16)], axis=1)
        if masked:
            q_pos = qi * tq + lax.broadcasted_iota(jnp.int32, (tq, tk), 0)
            k_pos = kj * tk + lax.broadcasted_iota(jnp.int32, (tq, tk), 1)
            visible = k_pos <= q_pos
        for branch, (m_ref, l_ref, a_ref) in enumerate(((m1_ref, l1_ref, a1_ref),
                                                        (m2_ref, l2_ref, a2_ref))):
            in_branch = (lane < dh) if branch == 0 else (lane >= dh)
            qb = jnp.where(in_branch, q, jnp.zeros_like(q))
            s = lax.dot_general(qb, k, (((1,), (1,)), ((), ())), preferred_element_type=F32)
            if masked:
                s = jnp.where(visible, s, -jnp.inf)
            m_prev = m_ref[...]
            m_new = jnp.maximum(m_prev, jnp.max(s, axis=-1, keepdims=True))
            alpha = jnp.exp(m_prev - m_new)
            p = jnp.exp(s - m_new).astype(BF16)
            pv = jnp.dot(p, v_aug, preferred_element_type=F32)
            a_ref[...] = alpha * a_ref[...] + pv[:, :dh2]
            l_ref[...] = alpha * l_ref[...] + pv[:, dh2:]
            m_ref[...] = m_new

    @pl.when(kj < qi)
    def _():
        step(False)

    @pl.when(kj == qi)
    def _():
        step(True)
        o1 = a1_ref[...] / l1_ref[...]
        o2 = a2_ref[...] / l2_ref[...]
        o_ref[...] = _diff_subln(o1, o2, lam_ref[0], g_ref[...], out_gain).astype(o_ref.dtype)


def _attn_prompt(qb, kb, vb, lam, subln_g, *, tq, out_gain):
    s, d = qb.shape
    dh2 = d // N_HEADS_B
    nq = s // tq
    qi = np.concatenate([np.full(i + 1, i, np.int32) for i in range(nq)])
    kj = np.concatenate([np.arange(i + 1, dtype=np.int32) for i in range(nq)])
    grid_spec = pltpu.PrefetchScalarGridSpec(
        num_scalar_prefetch=2,
        grid=(N_HEADS_B, len(qi)),
        in_specs=[pl.BlockSpec(memory_space=pltpu.SMEM),
                  pl.BlockSpec((tq, dh2), lambda h, t, qi, kj: (qi[t], h)),
                  pl.BlockSpec((tq, dh2), lambda h, t, qi, kj: (kj[t], h)),
                  pl.BlockSpec((tq, dh2), lambda h, t, qi, kj: (kj[t], h)),
                  pl.BlockSpec((1, dh2), lambda h, t, qi, kj: (0, 0))],
        out_specs=pl.BlockSpec((tq, dh2), lambda h, t, qi, kj: (qi[t], h)),
        scratch_shapes=[pltpu.VMEM((tq, 1), F32), pltpu.VMEM((tq, dh2), F32), pltpu.VMEM((tq, dh2), F32),
                        pltpu.VMEM((tq, 1), F32), pltpu.VMEM((tq, dh2), F32), pltpu.VMEM((tq, dh2), F32)],
    )
    return pl.pallas_call(
        functools.partial(_attn_prompt_kernel, out_gain=out_gain),
        grid_spec=grid_spec,
        out_shape=jax.ShapeDtypeStruct((s, d), BF16),
        compiler_params=_params("parallel", "arbitrary"),
        name="attn_prompt",
    )(jnp.asarray(qi), jnp.asarray(kj), lam, qb, kb, vb, subln_g)


def _attn_sample_kernel(pt_ref, lam_ref, q_ref, kn_ref, vn_ref, g_ref, *rest, pages_per_step, out_gain):
    k_refs = rest[:pages_per_step]
    v_refs = rest[pages_per_step:2 * pages_per_step]
    o_ref, qbd_ref, spread_ref, m_ref, l_ref, acc_ref = rest[2 * pages_per_step:]
    j = pl.program_id(1)
    n_rows, d = qbd_ref.shape
    n_heads = n_rows // 2
    page = spread_ref.shape[0]
    dh = d // n_rows

    @pl.when(j == 0)
    def _():
        r = lax.broadcasted_iota(jnp.int32, (n_rows, d), 0)
        c = lax.broadcasted_iota(jnp.int32, (n_rows, d), 1)
        own = (c // dh) == jnp.where(r < n_heads, 2 * r, 2 * (r - n_heads) + 1)
        q_rows = jnp.where(own, q_ref[...].astype(F32), 0.0)
        qbd_ref[...] = q_rows.astype(BF16)
        pos = lax.broadcasted_iota(jnp.int32, spread_ref.shape, 0)
        slot = lax.broadcasted_iota(jnp.int32, spread_ref.shape, 1)
        spread_ref[...] = jnp.where(slot // n_heads == pos, 1.0, 0.0).astype(BF16)
        m_ref[...] = jnp.sum(q_rows * kn_ref[...], axis=-1, keepdims=True)
        l_ref[...] = jnp.ones(l_ref.shape, F32)
        vn = vn_ref[...]
        acc_ref[...] = jnp.concatenate([vn, vn], axis=0)

    rr = lax.broadcasted_iota(jnp.int32, (n_rows, page * n_heads), 0)
    cc = lax.broadcasted_iota(jnp.int32, (n_rows, page * n_heads), 1)
    own_head = (cc % n_heads) == (rr % n_heads)
    for p in range(pages_per_step):
        kt = k_refs[p][...].astype(BF16)
        vp = v_refs[p][...].astype(BF16)
        s = jnp.dot(qbd_ref[...], kt, preferred_element_type=F32)
        m_prev = m_ref[...]
        m_new = jnp.maximum(m_prev, jnp.max(s, axis=-1, keepdims=True))
        alpha = jnp.exp(m_prev - m_new)
        pr = jnp.exp(s - m_new)
        l_ref[...] = alpha * l_ref[...] + jnp.sum(pr, axis=-1, keepdims=True)
        p_slots = jnp.dot(pr.astype(BF16), spread_ref[...], preferred_element_type=F32)
        p_own = jnp.where(own_head, p_slots, 0.0).astype(BF16)
        acc_ref[...] = alpha * acc_ref[...] + jnp.dot(p_own, vp, preferred_element_type=F32)
        m_ref[...] = m_new

    @pl.when(j == pl.num_programs(1) - 1)
    def _():
        o = acc_ref[...] / l_ref[...]
        o_ref[...] = _diff_subln(o[:n_heads], o[n_heads:], lam_ref[0], g_ref[...], out_gain
                                 ).astype(o_ref.dtype)


def _attn_sample(page_table, lam, qb, k_new, v_new, subln_g, cache_k, cache_v, *, pages_per_step,
                 out_gain):
    b, d = qb.shape
    n_pages = page_table.shape[1]
    n_phys, page, n_kh, dh = cache_k.shape
    n_vh, vd = cache_v.shape[2:]
    ck = jnp.transpose(cache_k, (0, 2, 3, 1)).reshape(n_phys, n_kh * dh, page)
    cv = cache_v.reshape(n_phys, page * n_vh, vd)
    row = lambda: pl.BlockSpec((None, 1, d), lambda i, j, pt: (i, 0, 0))

    def page_spec(p, rows, cols):
        return pl.BlockSpec((None, rows, cols),
                            lambda i, j, pt: (pt[i * n_pages + j * pages_per_step + p], 0, 0))

    grid_spec = pltpu.PrefetchScalarGridSpec(
        num_scalar_prefetch=1,
        grid=(b, n_pages // pages_per_step),
        in_specs=[pl.BlockSpec(memory_space=pltpu.SMEM), row(), row(),
                  pl.BlockSpec((None, n_vh, vd), lambda i, j, pt: (i, 0, 0)),
                  pl.BlockSpec((1, vd), lambda i, j, pt: (0, 0))]
                 + [page_spec(p, n_kh * dh, page) for p in range(pages_per_step)]
                 + [page_spec(p, page * n_vh, vd) for p in range(pages_per_step)],
        out_specs=pl.BlockSpec((None, n_vh, vd), lambda i, j, pt: (i, 0, 0)),
        scratch_shapes=[pltpu.VMEM((n_kh, d), BF16), pltpu.VMEM((page, page * n_vh), BF16),
                        pltpu.VMEM((n_kh, 1), F32), pltpu.VMEM((n_kh, 1), F32),
                        pltpu.VMEM((n_kh, vd), F32)],
    )
    out = pl.pallas_call(
        functools.partial(_attn_sample_kernel, pages_per_step=pages_per_step, out_gain=out_gain),
        grid_spec=grid_spec,
        out_shape=jax.ShapeDtypeStruct((b, n_vh, vd), BF16),
        compiler_params=_params("parallel", "arbitrary"),
        name="attn_sample",
    )(page_table.reshape(-1), lam, qb.reshape(b, 1, d), k_new.reshape(b, 1, d),
      v_new.reshape(b, n_vh, vd), subln_g, *([ck] * pages_per_step), *([cv] * pages_per_step))
    return out.reshape(b, d)


def _outproj_kernel(x_ref, o_ref, w_ref, y_ref):
    y_ref[...] = x_ref[...] + jnp.dot(o_ref[...], w_ref[...], preferred_element_type=F32)


def _outproj(x, o, w, *, tm):
    m, d = x.shape
    row = lambda i: (i, 0)
    return pl.pallas_call(
        _outproj_kernel,
        grid=(m // tm,),
        in_specs=[pl.BlockSpec((tm, d), row), pl.BlockSpec((tm, o.shape[1]), row), _resident(w.shape)],
        out_specs=pl.BlockSpec((tm, d), row),
        out_shape=jax.ShapeDtypeStruct((m, d), F32),
        compiler_params=_params("parallel"),
        name="attn_outproj",
    )(x, o, w)


def kernel(x_prompt, x_sample, cache_k, cache_v, page_table, a_norm_g, a_w_in, a_b_in, a_ln_g, a_ln_b,
           a_w_s, a_b_s, a_w_out, m_norm_g, m_w1, m_w2, kv_norm_g, w_k, w_v, b_norm_g, b_w_q, b_lq1,
           b_lk1, b_lq2, b_lk2, b_subln_g, b_w_o, final_norm_g):
    bp, sp, d = x_prompt.shape
    bs, ss, _ = x_sample.shape
    assert bp == 1 and ss == 1, "one prompt sequence and single-token sample rows"
    assert a_norm_g.shape[0] == 1 and b_norm_g.shape[0] == 1 and m_norm_g.shape[0] == 2
    n_kh, dh = cache_k.shape[2:]
    n_vh, vd = cache_v.shape[2:]
    dg = a_ln_g.shape[-1]
    gd = dg // N_GROUPS_A
    assert n_vh == N_HEADS_B and n_kh == 2 * N_HEADS_B and vd == 2 * dh == V7X_LANES and gd == V7X_LANES

    xp = x_prompt.reshape(sp, d)
    xs = x_sample.reshape(bs, d)
    vec = lambda a: a.reshape(1, -1).astype(F32)
    bf = lambda a: a.astype(BF16)

    mix_common = (vec(a_norm_g[0]), bf(a_w_in[0]), vec(a_b_in[0]), vec(a_ln_g[0]), vec(a_ln_b[0]))
    w_out = bf(a_w_out[0])
    per_group = lambda a: jnp.broadcast_to(a[..., None], a.shape + (gd,)).reshape(a.shape[:-1] + (dg,))
    bias_full = per_group(a_b_s[0].T)
    hp = _mixer_prompt(xp, *mix_common, a_w_s[0], bias_full, w_out, tm=256)
    scale_row = per_group(a_w_s[0][:, 0, 0]).reshape(1, dg)
    bias_row = bias_full[0:1]
    hs, chunk_v = _mixer_sample(xs, *mix_common, scale_row, bias_row, w_out)

    w1_0, w2_0 = bf(m_w1[0]), bf(m_w2[0])
    hp = _mlp(hp, vec(m_norm_g[0]), w1_0, w2_0, tm=512, tf=512)
    hs = _mlp(hs, vec(m_norm_g[0]), w1_0, w2_0, tm=bs, tf=512)

    q_scale = dh ** -0.5
    proj = (vec(kv_norm_g), vec(b_norm_g[0]), bf(w_k), bf(w_v), bf(b_w_q[0]))
    kp, vp, kpb, vpb, qpb = _kvq(hp, *proj, tm=256, q_scale=q_scale)
    ks, vs, _, _, qsb = _kvq(hs, *proj, tm=bs, q_scale=q_scale)

    lam_init = _lambda_init(1)
    f = lambda a: a.astype(F32)
    lam = (jnp.exp(jnp.sum(f(b_lq1[0]) * f(b_lk1[0]))) - jnp.exp(jnp.sum(f(b_lq2[0]) * f(b_lk2[0])))
           + lam_init).reshape(1).astype(F32)
    subln_g = vec(b_subln_g[0])
    op = _attn_prompt(qpb, kpb, vpb, lam, subln_g, tq=512, out_gain=1.0 - lam_init)
    os_ = _attn_sample(page_table, lam, qsb, ks, vs, subln_g, cache_k, cache_v, pages_per_step=4,
                       out_gain=1.0 - lam_init)
    w_o = bf(b_w_o[0])
    hp = _outproj(hp, op, w_o, tm=512)
    hs = _outproj(hs, os_, w_o, tm=bs)

    w1_1, w2_1 = bf(m_w1[1]), bf(m_w2[1])
    yp = _mlp(hp, vec(m_norm_g[1]), w1_1, w2_1, vec(final_norm_g), tm=512, tf=512)
    ys = _mlp(hs, vec(m_norm_g[1]), w1_1, w2_1, vec(final_norm_g), tm=bs, tf=512)

    return (yp.reshape(bp, sp, d), ys.reshape(bs, ss, d),
            kp.reshape(bp, sp, n_kh, dh), vp.reshape(bp, sp, n_vh, vd),
            ks.reshape(bs, ss, n_kh, dh), vs.reshape(bs, ss, n_vh, vd),
            chunk_v.reshape(1, bs, ss, dg))
```

```python
import functools
import math

import jax
import jax.numpy as jnp
import numpy as np
from jax import lax
from jax.experimental import pallas as pl
from jax.experimental.pallas import tpu as pltpu

F32 = jnp.float32
BF16 = jnp.bfloat16

CHUNK = 128
N_GROUPS_A = 16
N_HEADS_B = 16
RMS_EPS = 1e-5
LN_EPS = 1e-5
SQRT_HALF = math.sqrt(0.5)
LOG2_E = math.log2(math.e)

V7X_LANES = 128
V7X_VMEM_LIMIT_BYTES = 56 * 1024 * 1024


def _lambda_init(layer):
    return 0.8 - 0.6 * math.exp(-0.3 * layer)


def _rms_scale(x):
    return x * lax.rsqrt(jnp.mean(x * x, axis=-1, keepdims=True) + RMS_EPS)


def _gelu_exact(z):
    return 0.5 * z * (1.0 + lax.erf(z * SQRT_HALF))


def _resident(shape):
    return pl.BlockSpec(shape, lambda *_: (0,) * len(shape), pipeline_mode=pl.Buffered(1))


def _params(*semantics):
    return pltpu.CompilerParams(dimension_semantics=semantics,
                                vmem_limit_bytes=V7X_VMEM_LIMIT_BYTES)


def _mixer_front(x, g_ref, win_ref, bin_ref, lng_ref, lnb_ref):
    d_gate = lng_ref.shape[-1]
    h = (_rms_scale(x) * g_ref[...]).astype(BF16)
    z = jnp.dot(h, win_ref[...], preferred_element_type=F32) + bin_ref[...]
    z = _gelu_exact(z)
    u = z[:, :d_gate]
    v = z[:, d_gate:]
    mu = jnp.mean(v, axis=-1, keepdims=True)
    vc = v - mu
    var = jnp.mean(vc * vc, axis=-1, keepdims=True)
    vn = vc * lax.rsqrt(var + LN_EPS) * lng_ref[...] + lnb_ref[...]
    return u, vn


def _mixer_prompt_kernel(x_ref, g_ref, win_ref, bin_ref, lng_ref, lnb_ref, ws_ref, bias_ref,
                         wout_ref, o_ref):
    x = x_ref[...]
    u, vn = _mixer_front(x, g_ref, win_ref, bin_ref, lng_ref, lnb_ref)
    vb = vn.astype(BF16)
    n_chunks = x.shape[0] // CHUNK
    gd = vn.shape[1] // N_GROUPS_A
    t_pos = lax.broadcasted_iota(jnp.int32, (CHUNK, CHUNK), 0)
    s_pos = lax.broadcasted_iota(jnp.int32, (CHUNK, CHUNK), 1)
    causal = s_pos <= t_pos
    blocks = [[None] * N_GROUPS_A for _ in range(n_chunks)]
    for g in range(N_GROUPS_A):
        w_masked = jnp.where(causal, ws_ref[g], 0.0).astype(BF16)
        rhs = jnp.concatenate(
            [vb[c * CHUNK:(c + 1) * CHUNK, g * gd:(g + 1) * gd] for c in range(n_chunks)], axis=1)
        mg = jnp.dot(w_masked, rhs, preferred_element_type=F32)
        for c in range(n_chunks):
            blocks[c][g] = mg[:, c * gd:(c + 1) * gd]
    bias = bias_ref[...]
    mixed = jnp.concatenate(
        [jnp.concatenate(blocks[c], axis=1) + bias for c in range(n_chunks)], axis=0)
    t = (u * mixed).astype(BF16)
    o_ref[...] = x + jnp.dot(t, wout_ref[...], preferred_element_type=F32)


def _mixer_sample_kernel(x_ref, g_ref, win_ref, bin_ref, lng_ref, lnb_ref, scale_ref, bias_ref,
                         wout_ref, o_ref, vn_ref):
    x = x_ref[...]
    u, vn = _mixer_front(x, g_ref, win_ref, bin_ref, lng_ref, lnb_ref)
    vn_ref[...] = vn
    mixed = vn * scale_ref[...] + bias_ref[...]
    t = (u * mixed).astype(BF16)
    o_ref[...] = x + jnp.dot(t, wout_ref[...], preferred_element_type=F32)


def _mixer_prompt(x, norm_g, w_in, b_in, ln_g, ln_b, w_s, bias_full, w_out, *, tm):
    m, d = x.shape
    dg = ln_g.shape[-1]
    row = lambda i: (i, 0)
    return pl.pallas_call(
        _mixer_prompt_kernel,
        grid=(m // tm,),
        in_specs=[pl.BlockSpec((tm, d), row), _resident((1, d)), _resident((d, 2 * dg)),
                  _resident((1, 2 * dg)), _resident((1, dg)), _resident((1, dg)),
                  _resident(w_s.shape), _resident((CHUNK, dg)), _resident((dg, d))],
        out_specs=pl.BlockSpec((tm, d), row),
        out_shape=jax.ShapeDtypeStruct((m, d), F32),
        compiler_params=_params("parallel"),
        name="mixer_prompt",
    )(x, norm_g, w_in, b_in, ln_g, ln_b, w_s, bias_full, w_out)


def _mixer_sample(x, norm_g, w_in, b_in, ln_g, ln_b, scale_row, bias_row, w_out):
    m, d = x.shape
    dg = ln_g.shape[-1]
    return pl.pallas_call(
        _mixer_sample_kernel,
        grid=(1,),
        in_specs=[_resident((m, d)), _resident((1, d)), _resident((d, 2 * dg)),
                  _resident((1, 2 * dg)), _resident((1, dg)), _resident((1, dg)),
                  _resident((1, dg)), _resident((1, dg)), _resident((dg, d))],
        out_specs=[pl.BlockSpec((m, d), lambda i: (0, 0)), pl.BlockSpec((m, dg), lambda i: (0, 0))],
        out_shape=[jax.ShapeDtypeStruct((m, d), F32), jax.ShapeDtypeStruct((m, dg), F32)],
        compiler_params=_params("arbitrary"),
        name="mixer_sample",
    )(x, norm_g, w_in, b_in, ln_g, ln_b, scale_row, bias_row, w_out)


def _mlp_kernel(x_ref, g_ref, w1_ref, w2_ref, *rest, final_norm):
    if final_norm:
        fg_ref, o_ref, hn_ref, acc_ref = rest
    else:
        o_ref, hn_ref, acc_ref = rest
    j = pl.program_id(1)

    @pl.when(j == 0)
    def _():
        x = x_ref[...]
        hn_ref[...] = (_rms_scale(x) * g_ref[...]).astype(BF16)
        acc_ref[...] = x

    a = jnp.dot(hn_ref[...], w1_ref[...], preferred_element_type=F32)
    a = jnp.maximum(a, 0.0)
    a = (a * a).astype(BF16)
    acc_ref[...] += jnp.dot(a, w2_ref[...], preferred_element_type=F32)

    @pl.when(j == pl.num_programs(1) - 1)
    def _():
        y = acc_ref[...]
        if final_norm:
            y = _rms_scale(y) * fg_ref[...]
        o_ref[...] = y


def _mlp(x, g, w1, w2, final_g=None, *, tm, tf):
    m, d = x.shape
    f = w1.shape[1]
    final_norm = final_g is not None
    in_specs = [pl.BlockSpec((tm, d), lambda i, j: (i, 0)), _resident((1, d)),
                pl.BlockSpec((d, tf), lambda i, j: (0, j)), pl.BlockSpec((tf, d), lambda i, j: (j, 0))]
    args = [x, g, w1, w2]
    if final_norm:
        in_specs.append(_resident((1, d)))
        args.append(final_g)
    return pl.pallas_call(
        functools.partial(_mlp_kernel, final_norm=final_norm),
        grid=(m // tm, f // tf),
        in_specs=in_specs,
        out_specs=pl.BlockSpec((tm, d), lambda i, j: (i, 0)),
        out_shape=jax.ShapeDtypeStruct((m, d), F32),
        scratch_shapes=[pltpu.VMEM((tm, d), BF16), pltpu.VMEM((tm, d), F32)],
        compiler_params=_params("parallel", "arbitrary"),
        name="mlp_final" if final_norm else "mlp",
    )(*args)


def _kvq_kernel(x_ref, gk_ref, gq_ref, wk_ref, wv_ref, wq_ref, k_ref, v_ref, kb_ref, vb_ref, qb_ref,
                *, q_scale):
    xs = _rms_scale(x_ref[...])
    nk = (xs * gk_ref[...]).astype(BF16)
    nq = (xs * gq_ref[...]).astype(BF16)
    k = jnp.dot(nk, wk_ref[...], preferred_element_type=F32)
    v = jnp.dot(nk, wv_ref[...], preferred_element_type=F32)
    q = jnp.dot(nq, wq_ref[...], preferred_element_type=F32) * q_scale
    k_ref[...] = k
    v_ref[...] = v
    kb_ref[...] = k.astype(BF16)
    vb_ref[...] = v.astype(BF16)
    qb_ref[...] = q.astype(BF16)


def _kvq(x, gk, gq, wk, wv, wq, *, tm, q_scale):
    m, d = x.shape
    row = lambda i: (i, 0)
    blk = lambda: pl.BlockSpec((tm, d), row)
    return pl.pallas_call(
        functools.partial(_kvq_kernel, q_scale=q_scale),
        grid=(m // tm,),
        in_specs=[blk(), _resident((1, d)), _resident((1, d)),
                  _resident(wk.shape), _resident(wv.shape), _resident(wq.shape)],
        out_specs=[blk(), blk(), blk(), blk(), blk()],
        out_shape=[jax.ShapeDtypeStruct((m, d), F32), jax.ShapeDtypeStruct((m, d), F32),
                   jax.ShapeDtypeStruct((m, d), BF16), jax.ShapeDtypeStruct((m, d), BF16),
                   jax.ShapeDtypeStruct((m, d), BF16)],
        compiler_params=_params("parallel"),
        name="kvq_proj",
    )(x, gk, gq, wk, wv, wq)


def _diff_subln(o1, o2, lam, g, out_gain):
    o = o1 - lam * o2
    return _rms_scale(o) * g * out_gain


ONES_ROWS = 16


def _attn_prompt_kernel(lam_ref, q_ref, k_ref, v_ref, g_ref, o_ref, vt_ref, qt_ref, s0_ref, s1_ref, m_ref,
                        acc_ref, *, out_gain):
    qi = pl.program_id(1)
    tq, dh2 = q_ref.shape
    tk = tq
    dh = dh2 // 2
    seq = k_ref.shape[0]
    t_chunk = 512

    @pl.when(qi == 0)
    def _():
        for c in range(seq // t_chunk):
            cols = slice(c * t_chunk, (c + 1) * t_chunk)
            vt_ref[0:dh2, cols] = v_ref[cols, :].astype(F32).T.astype(BF16)
        vt_ref[dh2:, :] = jnp.ones((ONES_ROWS, seq), BF16)

    q = q_ref[...].astype(F32)
    lane = lax.broadcasted_iota(jnp.int32, (1, dh2), 1)
    q_both = jnp.concatenate([jnp.where(lane < dh, q, 0.0), jnp.where(lane >= dh, q, 0.0)], axis=0)
    qt_ref[...] = q_both.T.astype(BF16)
    m_ref[...] = jnp.full(m_ref.shape, -jnp.inf, F32)
    acc_ref[...] = jnp.zeros(acc_ref.shape, F32)

    s_refs = (s0_ref, s1_ref)

    def scores(c, slot):
        k = k_ref[pl.ds(pl.multiple_of(c * tk, tk), tk), :]
        s_refs[slot][...] = jnp.dot(k, qt_ref[...], preferred_element_type=F32)

    def consume(c, slot, masked):
        s = s_refs[slot][...]
        if masked:
            k_pos = c * tk + lax.broadcasted_iota(jnp.int32, (tk, 2 * tq), 0)
            col = lax.broadcasted_iota(jnp.int32, (tk, 2 * tq), 1)
            q_pos = qi * tq + jnp.where(col < tq, col, col - tq)
            s = jnp.where(k_pos <= q_pos, s, -jnp.inf)
        m_prev = m_ref[...]
        m_new = jnp.maximum(m_prev, jnp.max(s, axis=0, keepdims=True))
        alpha = jnp.exp2(m_prev - m_new)
        p = jnp.exp2(s - m_new).astype(BF16)
        vt = vt_ref[:, pl.ds(pl.multiple_of(c * tk, tk), tk)]
        acc_ref[...] = alpha * acc_ref[...] + jnp.dot(vt, p, preferred_element_type=F32)
        m_ref[...] = m_new

    scores(0, 0)

    def pair(c, carry):
        scores(2 * c + 1, 1)
        consume(2 * c, 0, False)
        scores(2 * c + 2, 0)
        consume(2 * c + 1, 1, False)
        return carry

    lax.fori_loop(0, qi // 2, pair, 0)

    @pl.when(qi % 2 == 0)
    def _():
        consume(qi, 0, True)

    @pl.when(qi % 2 == 1)
    def _():
        scores(qi, 1)
        consume(qi - 1, 0, False)
        consume(qi, 1, True)

    acc = acc_ref[...]
    o_t = acc[:dh2] / acc[dh2:dh2 + 1]
    d_t = o_t[:, :tq] - lam_ref[0] * o_t[:, tq:]
    o_ref[...] = (_rms_scale(d_t.T) * g_ref[...] * out_gain).astype(o_ref.dtype)


def _attn_prompt(qb, kb, vb, lam, subln_g, *, tq, out_gain):
    s, d = qb.shape
    dh2 = d // N_HEADS_B
    return pl.pallas_call(
        functools.partial(_attn_prompt_kernel, out_gain=out_gain),
        grid=(N_HEADS_B, s // tq),
        in_specs=[pl.BlockSpec(memory_space=pltpu.SMEM),
                  pl.BlockSpec((tq, dh2), lambda h, i: (i, h)),
                  pl.BlockSpec((s, dh2), lambda h, i: (0, h)),
                  pl.BlockSpec((s, dh2), lambda h, i: (0, h)),
                  pl.BlockSpec((1, dh2), lambda h, i: (0, 0))],
        out_specs=pl.BlockSpec((tq, dh2), lambda h, i: (i, h)),
        out_shape=jax.ShapeDtypeStruct((s, d), BF16),
        scratch_shapes=[pltpu.VMEM((dh2 + ONES_ROWS, s), BF16), pltpu.VMEM((dh2, 2 * tq), BF16),
                        pltpu.VMEM((tq, 2 * tq), F32), pltpu.VMEM((tq, 2 * tq), F32),
                        pltpu.VMEM((1, 2 * tq), F32), pltpu.VMEM((dh2 + ONES_ROWS, 2 * tq), F32)],
        compiler_params=_params("parallel", "arbitrary"),
        name="attn_prompt",
    )(lam, qb, kb, vb, subln_g)


def _attn_sample_kernel(pt_ref, lam_ref, q_ref, kn_ref, vn_ref, g_ref, *rest, pages_per_step, out_gain):
    k_refs = rest[:pages_per_step]
    v_refs = rest[pages_per_step:2 * pages_per_step]
    o_ref, qbd_ref, spread_ref, m_ref, l_ref, acc_ref = rest[2 * pages_per_step:]
    j = pl.program_id(1)
    n_rows, d = qbd_ref.shape
    n_heads = n_rows // 2
    page = spread_ref.shape[0]
    dh = d // n_rows

    @pl.when(j == 0)
    def _():
        r = lax.broadcasted_iota(jnp.int32, (n_rows, d), 0)
        c = lax.broadcasted_iota(jnp.int32, (n_rows, d), 1)
        own = (c // dh) == jnp.where(r < n_heads, 2 * r, 2 * (r - n_heads) + 1)
        q_rows = jnp.where(own, q_ref[...].astype(F32), 0.0)
        qbd_ref[...] = q_rows.astype(BF16)
        pos = lax.broadcasted_iota(jnp.int32, spread_ref.shape, 0)
        slot = lax.broadcasted_iota(jnp.int32, spread_ref.shape, 1)
        spread_ref[...] = jnp.where(slot // n_heads == pos, 1.0, 0.0).astype(BF16)
        m_ref[...] = jnp.sum(q_rows * kn_ref[...], axis=-1, keepdims=True)
        l_ref[...] = jnp.ones(l_ref.shape, F32)
        vn = vn_ref[...]
        acc_ref[...] = jnp.concatenate([vn, vn], axis=0)

    kt = jnp.concatenate([r[...].astype(BF16) for r in k_refs], axis=1)
    s = jnp.dot(qbd_ref[...], kt, preferred_element_type=F32)
    m_prev = m_ref[...]
    m_new = jnp.maximum(m_prev, jnp.max(s, axis=-1, keepdims=True))
    alpha = jnp.exp2(m_prev - m_new)
    pr = jnp.exp2(s - m_new)
    l_ref[...] = alpha * l_ref[...] + jnp.sum(pr, axis=-1, keepdims=True)
    m_ref[...] = m_new
    prb = pr.astype(BF16)
    p_rows = jnp.concatenate([prb[:, p * page:(p + 1) * page] for p in range(pages_per_step)], axis=0)
    p_slots = jnp.dot(p_rows, spread_ref[...], preferred_element_type=F32)
    rr = lax.broadcasted_iota(jnp.int32, p_slots.shape, 0)
    cc = lax.broadcasted_iota(jnp.int32, p_slots.shape, 1)
    p_own = jnp.where((cc % n_heads) == (rr % n_heads), p_slots, 0.0).astype(BF16)
    p_wide = jnp.concatenate([p_own[p * n_rows:(p + 1) * n_rows] for p in range(pages_per_step)], axis=1)
    vp = jnp.concatenate([r[...].astype(BF16) for r in v_refs], axis=0)
    acc_ref[...] = alpha * acc_ref[...] + jnp.dot(p_wide, vp, preferred_element_type=F32)

    @pl.when(j == pl.num_programs(1) - 1)
    def _():
        o = acc_ref[...] / l_ref[...]
        o_ref[...] = _diff_subln(o[:n_heads], o[n_heads:], lam_ref[0], g_ref[...], out_gain
                                 ).astype(o_ref.dtype)


def _attn_sample(page_table, lam, qb, k_new, v_new, subln_g, cache_k, cache_v, *, pages_per_step,
                 out_gain):
    b, d = qb.shape
    n_pages = page_table.shape[1]
    n_phys, page, n_kh, dh = cache_k.shape
    n_vh, vd = cache_v.shape[2:]
    ck = jnp.transpose(cache_k, (0, 2, 3, 1)).reshape(n_phys, n_kh * dh, page)
    cv = cache_v.reshape(n_phys, page * n_vh, vd)
    row = lambda: pl.BlockSpec((None, 1, d), lambda i, j, pt: (i, 0, 0))

    def page_spec(p, rows, cols):
        return pl.BlockSpec((None, rows, cols),
                            lambda i, j, pt: (pt[i * n_pages + j * pages_per_step + p], 0, 0))

    grid_spec = pltpu.PrefetchScalarGridSpec(
        num_scalar_prefetch=1,
        grid=(b, n_pages // pages_per_step),
        in_specs=[pl.BlockSpec(memory_space=pltpu.SMEM), row(), row(),
                  pl.BlockSpec((None, n_vh, vd), lambda i, j, pt: (i, 0, 0)),
                  pl.BlockSpec((1, vd), lambda i, j, pt: (0, 0))]
                 + [page_spec(p, n_kh * dh, page) for p in range(pages_per_step)]
                 + [page_spec(p, page * n_vh, vd) for p in range(pages_per_step)],
        out_specs=pl.BlockSpec((None, n_vh, vd), lambda i, j, pt: (i, 0, 0)),
        scratch_shapes=[pltpu.VMEM((n_kh, d), BF16), pltpu.VMEM((page, page * n_vh), BF16),
                        pltpu.VMEM((n_kh, 1), F32), pltpu.VMEM((n_kh, 1), F32),
                        pltpu.VMEM((n_kh, vd), F32)],
    )
    out = pl.pallas_call(
        functools.partial(_attn_sample_kernel, pages_per_step=pages_per_step, out_gain=out_gain),
        grid_spec=grid_spec,
        out_shape=jax.ShapeDtypeStruct((b, n_vh, vd), BF16),
        compiler_params=_params("parallel", "arbitrary"),
        name="attn_sample",
    )(page_table.reshape(-1), lam, qb.reshape(b, 1, d), k_new.reshape(b, 1, d),
      v_new.reshape(b, n_vh, vd), subln_g, *([ck] * pages_per_step), *([cv] * pages_per_step))
    return out.reshape(b, d)


def _outproj_kernel(x_ref, o_ref, w_ref, y_ref):
    y_ref[...] = x_ref[...] + jnp.dot(o_ref[...], w_ref[...], preferred_element_type=F32)


def _outproj(x, o, w, *, tm):
    m, d = x.shape
    row = lambda i: (i, 0)
    return pl.pallas_call(
        _outproj_kernel,
        grid=(m // tm,),
        in_specs=[pl.BlockSpec((tm, d), row), pl.BlockSpec((tm, o.shape[1]), row), _resident(w.shape)],
        out_specs=pl.BlockSpec((tm, d), row),
        out_shape=jax.ShapeDtypeStruct((m, d), F32),
        compiler_params=_params("parallel"),
        name="attn_outproj",
    )(x, o, w)


def kernel(x_prompt, x_sample, cache_k, cache_v, page_table, a_norm_g, a_w_in, a_b_in, a_ln_g, a_ln_b,
           a_w_s, a_b_s, a_w_out, m_norm_g, m_w1, m_w2, kv_norm_g, w_k, w_v, b_norm_g, b_w_q, b_lq1,
           b_lk1, b_lq2, b_lk2, b_subln_g, b_w_o, final_norm_g):
    bp, sp, d = x_prompt.shape
    bs, ss, _ = x_sample.shape
    assert bp == 1 and ss == 1, "one prompt sequence and single-token sample rows"
    assert a_norm_g.shape[0] == 1 and b_norm_g.shape[0] == 1 and m_norm_g.shape[0] == 2
    n_kh, dh = cache_k.shape[2:]
    n_vh, vd = cache_v.shape[2:]
    dg = a_ln_g.shape[-1]
    gd = dg // N_GROUPS_A
    assert n_vh == N_HEADS_B and n_kh == 2 * N_HEADS_B and vd == 2 * dh == V7X_LANES and gd == V7X_LANES

    xp = x_prompt.reshape(sp, d)
    xs = x_sample.reshape(bs, d)
    vec = lambda a: a.reshape(1, -1).astype(F32)
    bf = lambda a: a.astype(BF16)

    mix_common = (vec(a_norm_g[0]), bf(a_w_in[0]), vec(a_b_in[0]), vec(a_ln_g[0]), vec(a_ln_b[0]))
    w_out = bf(a_w_out[0])
    per_group = lambda a: jnp.broadcast_to(a[..., None], a.shape + (gd,)).reshape(a.shape[:-1] + (dg,))
    bias_full = per_group(a_b_s[0].T)
    hp = _mixer_prompt(xp, *mix_common, a_w_s[0], bias_full, w_out, tm=256)
    scale_row = per_group(a_w_s[0][:, 0, 0]).reshape(1, dg)
    bias_row = bias_full[0:1]
    hs, chunk_v = _mixer_sample(xs, *mix_common, scale_row, bias_row, w_out)

    w1_0, w2_0 = bf(m_w1[0]), bf(m_w2[0])
    hp = _mlp(hp, vec(m_norm_g[0]), w1_0, w2_0, tm=512, tf=512)
    hs = _mlp(hs, vec(m_norm_g[0]), w1_0, w2_0, tm=bs, tf=2048)

    q_scale = dh ** -0.5 * LOG2_E
    proj = (vec(kv_norm_g), vec(b_norm_g[0]), bf(w_k), bf(w_v), bf(b_w_q[0]))
    kp, vp, kpb, vpb, qpb = _kvq(hp, *proj, tm=256, q_scale=q_scale)
    ks, vs, _, _, qsb = _kvq(hs, *proj, tm=bs, q_scale=q_scale)

    lam_init = _lambda_init(1)
    f = lambda a: a.astype(F32)
    lam = (jnp.exp(jnp.sum(f(b_lq1[0]) * f(b_lk1[0]))) - jnp.exp(jnp.sum(f(b_lq2[0]) * f(b_lk2[0])))
           + lam_init).reshape(1).astype(F32)
    subln_g = vec(b_subln_g[0])
    op = _attn_prompt(qpb, kpb, vpb, lam, subln_g, tq=512, out_gain=1.0 - lam_init)
    os_ = _attn_sample(page_table, lam, qsb, ks, vs, subln_g, cache_k, cache_v, pages_per_step=4,
                       out_gain=1.0 - lam_init)
    w_o = bf(b_w_o[0])
    hp = _outproj(hp, op, w_o, tm=512)
    hs = _outproj(hs, os_, w_o, tm=bs)

    w1_1, w2_1 = bf(m_w1[1]), bf(m_w2[1])
    yp = _mlp(hp, vec(m_norm_g[1]), w1_1, w2_1, vec(final_norm_g), tm=512, tf=512)
    ys = _mlp(hs, vec(m_norm_g[1]), w1_1, w2_1, vec(final_norm_g), tm=bs, tf=2048)

    return (yp.reshape(bp, sp, d), ys.reshape(bs, ss, d),
            kp.reshape(bp, sp, n_kh, dh), vp.reshape(bp, sp, n_vh, vd),
            ks.reshape(bs, ss, n_kh, dh), vs.reshape(bs, ss, n_vh, vd),
            chunk_v.reshape(1, bs, ss, dg))
```

```python
import functools
import math

import jax
import jax.numpy as jnp
import numpy as np
from jax import lax
from jax.experimental import pallas as pl
from jax.experimental.pallas import tpu as pltpu

F32 = jnp.float32
BF16 = jnp.bfloat16

CHUNK = 128
N_GROUPS_A = 16
N_HEADS_B = 16
RMS_EPS = 1e-5
LN_EPS = 1e-5
SQRT_HALF = math.sqrt(0.5)
LOG2_E = math.log2(math.e)
DECODE_PAGES_PER_STEP = 4

V7X_LANES = 128
V7X_VMEM_LIMIT_BYTES = 56 * 1024 * 1024


def _lambda_init(layer):
    return 0.8 - 0.6 * math.exp(-0.3 * layer)


def _rms_scale(x):
    return x * lax.rsqrt(jnp.mean(x * x, axis=-1, keepdims=True) + RMS_EPS)


def _gelu_exact(z):
    return 0.5 * z * (1.0 + lax.erf(z * SQRT_HALF))


def _resident(shape):
    return pl.BlockSpec(shape, lambda *_: (0,) * len(shape), pipeline_mode=pl.Buffered(1))


def _params(*semantics):
    return pltpu.CompilerParams(dimension_semantics=semantics,
                                vmem_limit_bytes=V7X_VMEM_LIMIT_BYTES)


def _mixer_front(x, g_ref, win_ref, bin_ref, lng_ref, lnb_ref):
    d_gate = lng_ref.shape[-1]
    h = (_rms_scale(x) * g_ref[...]).astype(BF16)
    z = jnp.dot(h, win_ref[...], preferred_element_type=F32) + bin_ref[...]
    z = _gelu_exact(z)
    u = z[:, :d_gate]
    v = z[:, d_gate:]
    mu = jnp.mean(v, axis=-1, keepdims=True)
    vc = v - mu
    var = jnp.mean(vc * vc, axis=-1, keepdims=True)
    vn = vc * lax.rsqrt(var + LN_EPS) * lng_ref[...] + lnb_ref[...]
    return u, vn


def _mixer_prompt_kernel(x_ref, g_ref, win_ref, bin_ref, lng_ref, lnb_ref, ws_ref, bias_ref,
                         wout_ref, o_ref):
    x = x_ref[...]
    u, vn = _mixer_front(x, g_ref, win_ref, bin_ref, lng_ref, lnb_ref)
    vb = vn.astype(BF16)
    n_chunks = x.shape[0] // CHUNK
    gd = vn.shape[1] // N_GROUPS_A
    t_pos = lax.broadcasted_iota(jnp.int32, (CHUNK, CHUNK), 0)
    s_pos = lax.broadcasted_iota(jnp.int32, (CHUNK, CHUNK), 1)
    causal = s_pos <= t_pos
    blocks = [[None] * N_GROUPS_A for _ in range(n_chunks)]
    for g in range(N_GROUPS_A):
        w_masked = jnp.where(causal, ws_ref[g], 0.0).astype(BF16)
        rhs = jnp.concatenate(
            [vb[c * CHUNK:(c + 1) * CHUNK, g * gd:(g + 1) * gd] for c in range(n_chunks)], axis=1)
        mg = jnp.dot(w_masked, rhs, preferred_element_type=F32)
        for c in range(n_chunks):
            blocks[c][g] = mg[:, c * gd:(c + 1) * gd]
    bias = bias_ref[...]
    mixed = jnp.concatenate(
        [jnp.concatenate(blocks[c], axis=1) + bias for c in range(n_chunks)], axis=0)
    t = (u * mixed).astype(BF16)
    o_ref[...] = x + jnp.dot(t, wout_ref[...], preferred_element_type=F32)


def _mixer_sample_kernel(x_ref, g_ref, win_ref, bin_ref, lng_ref, lnb_ref, scale_ref, bias_ref,
                         wout_ref, o_ref, vn_ref):
    x = x_ref[...]
    u, vn = _mixer_front(x, g_ref, win_ref, bin_ref, lng_ref, lnb_ref)
    vn_ref[...] = vn
    mixed = vn * scale_ref[...] + bias_ref[...]
    t = (u * mixed).astype(BF16)
    o_ref[...] = x + jnp.dot(t, wout_ref[...], preferred_element_type=F32)


def _mixer_prompt(x, norm_g, w_in, b_in, ln_g, ln_b, w_s, bias_full, w_out, *, tm):
    m, d = x.shape
    dg = ln_g.shape[-1]
    row = lambda i: (i, 0)
    return pl.pallas_call(
        _mixer_prompt_kernel,
        grid=(m // tm,),
        in_specs=[pl.BlockSpec((tm, d), row), _resident((1, d)), _resident((d, 2 * dg)),
                  _resident((1, 2 * dg)), _resident((1, dg)), _resident((1, dg)),
                  _resident(w_s.shape), _resident((CHUNK, dg)), _resident((dg, d))],
        out_specs=pl.BlockSpec((tm, d), row),
        out_shape=jax.ShapeDtypeStruct((m, d), F32),
        compiler_params=_params("parallel"),
        name="mixer_prompt",
    )(x, norm_g, w_in, b_in, ln_g, ln_b, w_s, bias_full, w_out)


def _mixer_sample(x, norm_g, w_in, b_in, ln_g, ln_b, scale_row, bias_row, w_out):
    m, d = x.shape
    dg = ln_g.shape[-1]
    return pl.pallas_call(
        _mixer_sample_kernel,
        grid=(1,),
        in_specs=[_resident((m, d)), _resident((1, d)), _resident((d, 2 * dg)),
                  _resident((1, 2 * dg)), _resident((1, dg)), _resident((1, dg)),
                  _resident((1, dg)), _resident((1, dg)), _resident((dg, d))],
        out_specs=[pl.BlockSpec((m, d), lambda i: (0, 0)), pl.BlockSpec((m, dg), lambda i: (0, 0))],
        out_shape=[jax.ShapeDtypeStruct((m, d), F32), jax.ShapeDtypeStruct((m, dg), F32)],
        compiler_params=_params("arbitrary"),
        name="mixer_sample",
    )(x, norm_g, w_in, b_in, ln_g, ln_b, scale_row, bias_row, w_out)


def _mlp_kernel(*refs, final_norm, decode_pages, out_gain):
    refs = list(refs)
    if decode_pages:
        refs.pop(0)
    x_ref, g_ref, w1_ref, w2_ref = refs[:4]
    refs = refs[4:]
    fg_ref = refs.pop(0) if final_norm else None
    if decode_pages:
        lam_ref, q_ref, kn_ref, vn_ref, sg_ref = refs[:5]
        k_refs = refs[5:5 + decode_pages]
        v_refs = refs[5 + decode_pages:5 + 2 * decode_pages]
        o_ref, od_ref, hn_ref, qbd_ref, spread_ref, m_ref, l_ref, dacc_ref = refs[5 + 2 * decode_pages:]
        n_rows, d = qbd_ref.shape
        n_heads = n_rows // 2
        page = spread_ref.shape[0]
        dh = d // n_rows
    else:
        o_ref, hn_ref = refs
    j = pl.program_id(1)

    @pl.when(j == 0)
    def _():
        x = x_ref[...]
        hn_ref[...] = (_rms_scale(x) * g_ref[...]).astype(BF16)
        o_ref[...] = x
        if decode_pages:
            r = lax.broadcasted_iota(jnp.int32, (n_rows, d), 0)
            c = lax.broadcasted_iota(jnp.int32, (n_rows, d), 1)
            own = (c // dh) == jnp.where(r < n_heads, 2 * r, 2 * (r - n_heads) + 1)
            q_rows = jnp.where(own, q_ref[...].astype(F32), 0.0)
            qbd_ref[...] = q_rows.astype(BF16)
            pos = lax.broadcasted_iota(jnp.int32, spread_ref.shape, 0)
            slot = lax.broadcasted_iota(jnp.int32, spread_ref.shape, 1)
            spread_ref[...] = jnp.where(slot // n_heads == pos, 1.0, 0.0).astype(BF16)
            m_ref[...] = jnp.sum(q_rows * kn_ref[...], axis=-1, keepdims=True)
            l_ref[...] = jnp.ones(l_ref.shape, F32)
            vn = vn_ref[...]
            dacc_ref[...] = jnp.concatenate([vn, vn], axis=0)

    if decode_pages:
        kt = jnp.concatenate([r[...].astype(BF16) for r in k_refs], axis=1)
        s = jnp.dot(qbd_ref[...], kt, preferred_element_type=F32)
    a = jnp.dot(hn_ref[...], w1_ref[...], preferred_element_type=F32)
    if decode_pages:
        m_prev = m_ref[...]
        m_new = jnp.maximum(m_prev, jnp.max(s, axis=-1, keepdims=True))
        alpha = jnp.exp2(m_prev - m_new)
        pr = jnp.exp2(s - m_new)
        l_ref[...] = alpha * l_ref[...] + jnp.sum(pr, axis=-1, keepdims=True)
        m_ref[...] = m_new
        prb = pr.astype(BF16)
        p_rows = jnp.concatenate([prb[:, p * page:(p + 1) * page] for p in range(decode_pages)], axis=0)
        p_slots = jnp.dot(p_rows, spread_ref[...], preferred_element_type=F32)
    a = jnp.maximum(a, 0.0)
    a = (a * a).astype(BF16)
    o_ref[...] += jnp.dot(a, w2_ref[...], preferred_element_type=F32)
    if decode_pages:
        rr = lax.broadcasted_iota(jnp.int32, p_slots.shape, 0)
        cc = lax.broadcasted_iota(jnp.int32, p_slots.shape, 1)
        p_own = jnp.where((cc % n_heads) == (rr % n_heads), p_slots, 0.0).astype(BF16)
        p_wide = jnp.concatenate([p_own[p * n_rows:(p + 1) * n_rows] for p in range(decode_pages)], axis=1)
        vp = jnp.concatenate([r[...].astype(BF16) for r in v_refs], axis=0)
        dacc_ref[...] = alpha * dacc_ref[...] + jnp.dot(p_wide, vp, preferred_element_type=F32)

    @pl.when(j == pl.num_programs(1) - 1)
    def _():
        if final_norm:
            o_ref[...] = _rms_scale(o_ref[...]) * fg_ref[...]
        if decode_pages:
            o = dacc_ref[...] / l_ref[...]
            od_ref[...] = _diff_subln(o[:n_heads], o[n_heads:], lam_ref[0], sg_ref[...], out_gain
                                      ).astype(od_ref.dtype)


def _mlp(x, g, w1, w2, final_g=None, decode=None, *, layer, tm, tf):
    m, d = x.shape
    f = w1.shape[2]
    final_norm = final_g is not None
    grid = (m // tm, f // tf)
    in_specs = [pl.BlockSpec((tm, d), lambda i, j, *_: (i, 0)), _resident((1, d)),
                pl.BlockSpec((None, d, tf), lambda i, j, *_: (layer, 0, j)),
                pl.BlockSpec((None, tf, d), lambda i, j, *_: (layer, j, 0))]
    args = [x, g, w1, w2]
    out_specs = [pl.BlockSpec((tm, d), lambda i, j, *_: (i, 0))]
    out_shape = [jax.ShapeDtypeStruct((m, d), F32)]
    scratch = [pltpu.VMEM((tm, d), BF16)]
    prefetch = []
    pages_per_step = 0
    out_gain = None
    if final_norm:
        in_specs.append(_resident((1, d)))
        args.append(final_g)
    if decode is not None:
        page_table, first_seq, lam, qb, k_new, v_new, subln_g, cache_k, cache_v, out_gain = decode
        n_seq, n_pages = page_table.shape
        n_phys, page, n_kh, dh = cache_k.shape
        n_vh, vd = cache_v.shape[2:]
        pages_per_step = n_pages // grid[1]
        assert first_seq + grid[0] <= n_seq and pages_per_step * grid[1] == n_pages
        ck = jnp.transpose(cache_k, (0, 2, 3, 1)).reshape(n_phys, n_kh * dh, page)
        cv = cache_v.reshape(n_phys, page * n_vh, vd)
        row = lambda: pl.BlockSpec((None, 1, d), lambda i, j, pt: (first_seq + i, 0, 0))

        def page_spec(p, rows, cols):
            return pl.BlockSpec(
                (None, rows, cols),
                lambda i, j, pt: (pt[(first_seq + i) * n_pages + j * pages_per_step + p], 0, 0))

        prefetch = [page_table.reshape(-1)]
        in_specs += ([pl.BlockSpec(memory_space=pltpu.SMEM), row(), row(),
                      pl.BlockSpec((None, n_vh, vd), lambda i, j, pt: (first_seq + i, 0, 0)),
                      pl.BlockSpec((1, vd), lambda i, j, pt: (0, 0))]
                     + [page_spec(p, n_kh * dh, page) for p in range(pages_per_step)]
                     + [page_spec(p, page * n_vh, vd) for p in range(pages_per_step)])
        args += [lam, qb.reshape(n_seq, 1, d), k_new.reshape(n_seq, 1, d), v_new.reshape(n_seq, n_vh, vd),
                 subln_g] + [ck] * pages_per_step + [cv] * pages_per_step
        out_specs.append(pl.BlockSpec((None, n_vh, vd), lambda i, j, pt: (i, 0, 0)))
        out_shape.append(jax.ShapeDtypeStruct((grid[0], n_vh, vd), BF16))
        scratch += [pltpu.VMEM((n_kh, d), BF16), pltpu.VMEM((page, page * n_vh), BF16),
                    pltpu.VMEM((n_kh, 1), F32), pltpu.VMEM((n_kh, 1), F32), pltpu.VMEM((n_kh, vd), F32)]
    out = pl.pallas_call(
        functools.partial(_mlp_kernel, final_norm=final_norm, decode_pages=pages_per_step,
                          out_gain=out_gain),
        grid_spec=pltpu.PrefetchScalarGridSpec(
            num_scalar_prefetch=len(prefetch), grid=grid, in_specs=in_specs, out_specs=out_specs,
            scratch_shapes=scratch),
        out_shape=out_shape,
        compiler_params=_params("parallel", "arbitrary"),
        name=("mlp_final" if final_norm else "mlp") + ("_decode" if decode is not None else ""),
    )(*prefetch, *args)
    if decode is None:
        return out[0]
    return out[0], out[1].reshape(grid[0], n_vh * vd)


def _kvq_kernel(x_ref, gk_ref, gq_ref, wk_ref, wv_ref, wq_ref, k_ref, v_ref, kb_ref, vb_ref, qb_ref,
                *, q_scale):
    xs = _rms_scale(x_ref[...])
    nk = (xs * gk_ref[...]).astype(BF16)
    nq = (xs * gq_ref[...]).astype(BF16)
    k = jnp.dot(nk, wk_ref[...], preferred_element_type=F32)
    v = jnp.dot(nk, wv_ref[...], preferred_element_type=F32)
    q = jnp.dot(nq, wq_ref[...], preferred_element_type=F32) * q_scale
    k_ref[...] = k
    v_ref[...] = v
    kb_ref[...] = k.astype(BF16)
    vb_ref[...] = v.astype(BF16)
    qb_ref[...] = q.astype(BF16)


def _kvq(x, gk, gq, wk, wv, wq, *, tm, q_scale):
    m, d = x.shape
    row = lambda i: (i, 0)
    blk = lambda: pl.BlockSpec((tm, d), row)
    return pl.pallas_call(
        functools.partial(_kvq_kernel, q_scale=q_scale),
        grid=(m // tm,),
        in_specs=[blk(), _resident((1, d)), _resident((1, d)),
                  _resident(wk.shape), _resident(wv.shape), _resident(wq.shape)],
        out_specs=[blk(), blk(), blk(), blk(), blk()],
        out_shape=[jax.ShapeDtypeStruct((m, d), F32), jax.ShapeDtypeStruct((m, d), F32),
                   jax.ShapeDtypeStruct((m, d), BF16), jax.ShapeDtypeStruct((m, d), BF16),
                   jax.ShapeDtypeStruct((m, d), BF16)],
        compiler_params=_params("parallel"),
        name="kvq_proj",
    )(x, gk, gq, wk, wv, wq)


def _diff_subln(o1, o2, lam, g, out_gain):
    o = o1 - lam * o2
    return _rms_scale(o) * g * out_gain


ONES_ROWS = 16


def _attn_prompt_kernel(lam_ref, q_ref, k_ref, v_ref, g_ref, o_ref, vt_ref, qt_ref, s0_ref, s1_ref, m_ref,
                        acc_ref, *, out_gain):
    qi = pl.program_id(1)
    tq, dh2 = q_ref.shape
    tk = tq
    dh = dh2 // 2
    seq = k_ref.shape[0]
    t_chunk = 512

    @pl.when(qi == 0)
    def _():
        for c in range(seq // t_chunk):
            cols = slice(c * t_chunk, (c + 1) * t_chunk)
            vt_ref[0:dh2, cols] = v_ref[cols, :].astype(F32).T.astype(BF16)
        vt_ref[dh2:, :] = jnp.ones((ONES_ROWS, seq), BF16)

    q = q_ref[...].astype(F32)
    lane = lax.broadcasted_iota(jnp.int32, (1, dh2), 1)
    q_both = jnp.concatenate([jnp.where(lane < dh, q, 0.0), jnp.where(lane >= dh, q, 0.0)], axis=0)
    qt_ref[...] = q_both.T.astype(BF16)
    m_ref[...] = jnp.full(m_ref.shape, -jnp.inf, F32)
    acc_ref[...] = jnp.zeros(acc_ref.shape, F32)

    s_refs = (s0_ref, s1_ref)

    def scores(c, slot):
        k = k_ref[pl.ds(pl.multiple_of(c * tk, tk), tk), :]
        s_refs[slot][...] = jnp.dot(k, qt_ref[...], preferred_element_type=F32)

    def consume(c, slot, masked):
        s = s_refs[slot][...]
        if masked:
            k_pos = c * tk + lax.broadcasted_iota(jnp.int32, (tk, 2 * tq), 0)
            col = lax.broadcasted_iota(jnp.int32, (tk, 2 * tq), 1)
            q_pos = qi * tq + jnp.where(col < tq, col, col - tq)
            s = jnp.where(k_pos <= q_pos, s, -jnp.inf)
        m_prev = m_ref[...]
        m_new = jnp.maximum(m_prev, jnp.max(s, axis=0, keepdims=True))
        alpha = jnp.exp2(m_prev - m_new)
        p = jnp.exp2(s - m_new).astype(BF16)
        vt = vt_ref[:, pl.ds(pl.multiple_of(c * tk, tk), tk)]
        acc_ref[...] = alpha * acc_ref[...] + jnp.dot(vt, p, preferred_element_type=F32)
        m_ref[...] = m_new

    scores(0, 0)

    def pair(c, carry):
        scores(2 * c + 1, 1)
        consume(2 * c, 0, False)
        scores(2 * c + 2, 0)
        consume(2 * c + 1, 1, False)
        return carry

    lax.fori_loop(0, qi // 2, pair, 0)

    @pl.when(qi % 2 == 0)
    def _():
        consume(qi, 0, True)

    @pl.when(qi % 2 == 1)
    def _():
        scores(qi, 1)
        consume(qi - 1, 0, False)
        consume(qi, 1, True)

    acc = acc_ref[...]
    o_t = acc[:dh2] / acc[dh2:dh2 + 1]
    d_t = o_t[:, :tq] - lam_ref[0] * o_t[:, tq:]
    o_ref[...] = (_rms_scale(d_t.T) * g_ref[...] * out_gain).astype(o_ref.dtype)


def _attn_prompt(qb, kb, vb, lam, subln_g, *, tq, out_gain):
    s, d = qb.shape
    dh2 = d // N_HEADS_B
    return pl.pallas_call(
        functools.partial(_attn_prompt_kernel, out_gain=out_gain),
        grid=(N_HEADS_B, s // tq),
        in_specs=[pl.BlockSpec(memory_space=pltpu.SMEM),
                  pl.BlockSpec((tq, dh2), lambda h, i: (i, h)),
                  pl.BlockSpec((s, dh2), lambda h, i: (0, h)),
                  pl.BlockSpec((s, dh2), lambda h, i: (0, h)),
                  pl.BlockSpec((1, dh2), lambda h, i: (0, 0))],
        out_specs=pl.BlockSpec((tq, dh2), lambda h, i: (i, h)),
        out_shape=jax.ShapeDtypeStruct((s, d), BF16),
        scratch_shapes=[pltpu.VMEM((dh2 + ONES_ROWS, s), BF16), pltpu.VMEM((dh2, 2 * tq), BF16),
                        pltpu.VMEM((tq, 2 * tq), F32), pltpu.VMEM((tq, 2 * tq), F32),
                        pltpu.VMEM((1, 2 * tq), F32), pltpu.VMEM((dh2 + ONES_ROWS, 2 * tq), F32)],
        compiler_params=_params("parallel", "arbitrary"),
        name="attn_prompt",
    )(lam, qb, kb, vb, subln_g)


def _outproj_kernel(x_ref, o_ref, w_ref, y_ref):
    y_ref[...] = x_ref[...] + jnp.dot(o_ref[...], w_ref[...], preferred_element_type=F32)


def _outproj(x, o, w, *, tm):
    m, d = x.shape
    row = lambda i: (i, 0)
    return pl.pallas_call(
        _outproj_kernel,
        grid=(m // tm,),
        in_specs=[pl.BlockSpec((tm, d), row), pl.BlockSpec((tm, o.shape[1]), row), _resident(w.shape)],
        out_specs=pl.BlockSpec((tm, d), row),
        out_shape=jax.ShapeDtypeStruct((m, d), F32),
        compiler_params=_params("parallel"),
        name="attn_outproj",
    )(x, o, w)


def kernel(x_prompt, x_sample, cache_k, cache_v, page_table, a_norm_g, a_w_in, a_b_in, a_ln_g, a_ln_b,
           a_w_s, a_b_s, a_w_out, m_norm_g, m_w1, m_w2, kv_norm_g, w_k, w_v, b_norm_g, b_w_q, b_lq1,
           b_lk1, b_lq2, b_lk2, b_subln_g, b_w_o, final_norm_g):
    bp, sp, d = x_prompt.shape
    bs, ss, _ = x_sample.shape
    assert bp == 1 and ss == 1, "one prompt sequence and single-token sample rows"
    assert a_norm_g.shape[0] == 1 and b_norm_g.shape[0] == 1 and m_norm_g.shape[0] == 2
    n_kh, dh = cache_k.shape[2:]
    n_vh, vd = cache_v.shape[2:]
    dg = a_ln_g.shape[-1]
    gd = dg // N_GROUPS_A
    assert n_vh == N_HEADS_B and n_kh == 2 * N_HEADS_B and vd == 2 * dh == V7X_LANES and gd == V7X_LANES

    xp = x_prompt.reshape(sp, d)
    xs = x_sample.reshape(bs, d)
    vec = lambda a: a.reshape(1, -1).astype(F32)
    bf = lambda a: a.astype(BF16)

    mix_common = (vec(a_norm_g[0]), bf(a_w_in[0]), vec(a_b_in[0]), vec(a_ln_g[0]), vec(a_ln_b[0]))
    w_out = bf(a_w_out[0])
    per_group = lambda a: jnp.broadcast_to(a[..., None], a.shape + (gd,)).reshape(a.shape[:-1] + (dg,))
    bias_full = per_group(a_b_s[0].T)
    hp = _mixer_prompt(xp, *mix_common, a_w_s[0], bias_full, w_out, tm=256)
    scale_row = per_group(a_w_s[0][:, 0, 0]).reshape(1, dg)
    bias_row = bias_full[0:1]
    hs, chunk_v = _mixer_sample(xs, *mix_common, scale_row, bias_row, w_out)

    w1, w2 = bf(m_w1), bf(m_w2)
    hs = _mlp(hs, vec(m_norm_g[0]), w1, w2, layer=0, tm=bs, tf=2048)
    q_scale = dh ** -0.5 * LOG2_E
    proj = (vec(kv_norm_g), vec(b_norm_g[0]), bf(w_k), bf(w_v), bf(b_w_q[0]))
    ks, vs, _, _, qsb = _kvq(hs, *proj, tm=bs, q_scale=q_scale)

    lam_init = _lambda_init(1)
    f = lambda a: a.astype(F32)
    lam = (jnp.exp(jnp.sum(f(b_lq1[0]) * f(b_lk1[0]))) - jnp.exp(jnp.sum(f(b_lq2[0]) * f(b_lk2[0])))
           + lam_init).reshape(1).astype(F32)
    subln_g = vec(b_subln_g[0])
    out_gain = 1.0 - lam_init

    mlp_tm = sp // (bs // 2)
    mlp_tf = m_w1.shape[2] // (page_table.shape[1] // DECODE_PAGES_PER_STEP)
    decode = lambda first: (page_table, first, lam, qsb, ks, vs, subln_g, cache_k, cache_v, out_gain)

    hp, os_a = _mlp(hp, vec(m_norm_g[0]), w1, w2, None, decode(0), layer=0, tm=mlp_tm, tf=mlp_tf)

    kp, vp, kpb, vpb, qpb = _kvq(hp, *proj, tm=256, q_scale=q_scale)
    op = _attn_prompt(qpb, kpb, vpb, lam, subln_g, tq=512, out_gain=out_gain)
    w_o = bf(b_w_o[0])
    hp = _outproj(hp, op, w_o, tm=512)

    yp, os_b = _mlp(hp, vec(m_norm_g[1]), w1, w2, vec(final_norm_g), decode(bs // 2),
                    layer=1, tm=mlp_tm, tf=mlp_tf)

    hs = _outproj(hs, jnp.concatenate([os_a, os_b], axis=0), w_o, tm=bs)
    ys = _mlp(hs, vec(m_norm_g[1]), w1, w2, vec(final_norm_g), layer=1, tm=bs, tf=2048)

    return (yp.reshape(bp, sp, d), ys.reshape(bs, ss, d),
            kp.reshape(bp, sp, n_kh, dh), vp.reshape(bp, sp, n_vh, vd),
            ks.reshape(bs, ss, n_kh, dh), vs.reshape(bs, ss, n_vh, vd),
            chunk_v.reshape(1, bs, ss, dg))
```

```python
import functools
import math

import jax
import jax.numpy as jnp
import numpy as np
from jax import lax
from jax.experimental import pallas as pl
from jax.experimental.pallas import tpu as pltpu

F32 = jnp.float32
BF16 = jnp.bfloat16

CHUNK = 128
N_GROUPS_A = 16
N_HEADS_B = 16
RMS_EPS = 1e-5
LN_EPS = 1e-5
SQRT_HALF = math.sqrt(0.5)
LOG2_E = math.log2(math.e)
DECODE_PAGES_PER_STEP = 4
MLP_ROW_GROUP = 2

V7X_LANES = 128
V7X_VMEM_LIMIT_BYTES = 56 * 1024 * 1024


def _lambda_init(layer):
    return 0.8 - 0.6 * math.exp(-0.3 * layer)


def _rms_scale(x):
    return x * lax.rsqrt(jnp.mean(x * x, axis=-1, keepdims=True) + RMS_EPS)


def _gelu_exact(z):
    return 0.5 * z * (1.0 + lax.erf(z * SQRT_HALF))


def _resident(shape):
    return pl.BlockSpec(shape, lambda *_: (0,) * len(shape), pipeline_mode=pl.Buffered(1))


def _params(*semantics):
    return pltpu.CompilerParams(dimension_semantics=semantics,
                                vmem_limit_bytes=V7X_VMEM_LIMIT_BYTES)


def _mixer_front(x, g_ref, win_ref, bin_ref, lng_ref, lnb_ref):
    d_gate = lng_ref.shape[-1]
    h = (_rms_scale(x) * g_ref[...]).astype(BF16)
    z = jnp.dot(h, win_ref[...], preferred_element_type=F32) + bin_ref[...]
    z = _gelu_exact(z)
    u = z[:, :d_gate]
    v = z[:, d_gate:]
    mu = jnp.mean(v, axis=-1, keepdims=True)
    vc = v - mu
    var = jnp.mean(vc * vc, axis=-1, keepdims=True)
    vn = vc * lax.rsqrt(var + LN_EPS) * lng_ref[...] + lnb_ref[...]
    return u, vn


def _mixer_prompt_kernel(x_ref, g_ref, win_ref, bin_ref, lng_ref, lnb_ref, ws_ref, bias_ref,
                         wout_ref, o_ref):
    x = x_ref[...]
    u, vn = _mixer_front(x, g_ref, win_ref, bin_ref, lng_ref, lnb_ref)
    vb = vn.astype(BF16)
    n_chunks = x.shape[0] // CHUNK
    gd = vn.shape[1] // N_GROUPS_A
    t_pos = lax.broadcasted_iota(jnp.int32, (CHUNK, CHUNK), 0)
    s_pos = lax.broadcasted_iota(jnp.int32, (CHUNK, CHUNK), 1)
    causal = s_pos <= t_pos
    blocks = [[None] * N_GROUPS_A for _ in range(n_chunks)]
    for g in range(N_GROUPS_A):
        w_masked = jnp.where(causal, ws_ref[g], 0.0).astype(BF16)
        rhs = jnp.concatenate(
            [vb[c * CHUNK:(c + 1) * CHUNK, g * gd:(g + 1) * gd] for c in range(n_chunks)], axis=1)
        mg = jnp.dot(w_masked, rhs, preferred_element_type=F32)
        for c in range(n_chunks):
            blocks[c][g] = mg[:, c * gd:(c + 1) * gd]
    bias = bias_ref[...]
    mixed = jnp.concatenate(
        [jnp.concatenate(blocks[c], axis=1) + bias for c in range(n_chunks)], axis=0)
    t = (u * mixed).astype(BF16)
    o_ref[...] = x + jnp.dot(t, wout_ref[...], preferred_element_type=F32)


def _mixer_sample_kernel(x_ref, g_ref, win_ref, bin_ref, lng_ref, lnb_ref, scale_ref, bias_ref,
                         wout_ref, o_ref, vn_ref):
    x = x_ref[...]
    u, vn = _mixer_front(x, g_ref, win_ref, bin_ref, lng_ref, lnb_ref)
    vn_ref[...] = vn
    mixed = vn * scale_ref[...] + bias_ref[...]
    t = (u * mixed).astype(BF16)
    o_ref[...] = x + jnp.dot(t, wout_ref[...], preferred_element_type=F32)


def _mixer_prompt(x, norm_g, w_in, b_in, ln_g, ln_b, w_s, bias_full, w_out, *, tm):
    m, d = x.shape
    dg = ln_g.shape[-1]
    row = lambda i: (i, 0)
    return pl.pallas_call(
        _mixer_prompt_kernel,
        grid=(m // tm,),
        in_specs=[pl.BlockSpec((tm, d), row), _resident((1, d)), _resident((d, 2 * dg)),
                  _resident((1, 2 * dg)), _resident((1, dg)), _resident((1, dg)),
                  _resident(w_s.shape), _resident((CHUNK, dg)), _resident((dg, d))],
        out_specs=pl.BlockSpec((tm, d), row),
        out_shape=jax.ShapeDtypeStruct((m, d), F32),
        compiler_params=_params("parallel"),
        name="mixer_prompt",
    )(x, norm_g, w_in, b_in, ln_g, ln_b, w_s, bias_full, w_out)


def _mixer_sample(x, norm_g, w_in, b_in, ln_g, ln_b, scale_row, bias_row, w_out):
    m, d = x.shape
    dg = ln_g.shape[-1]
    return pl.pallas_call(
        _mixer_sample_kernel,
        grid=(1,),
        in_specs=[_resident((m, d)), _resident((1, d)), _resident((d, 2 * dg)),
                  _resident((1, 2 * dg)), _resident((1, dg)), _resident((1, dg)),
                  _resident((1, dg)), _resident((1, dg)), _resident((dg, d))],
        out_specs=[pl.BlockSpec((m, d), lambda i: (0, 0)), pl.BlockSpec((m, dg), lambda i: (0, 0))],
        out_shape=[jax.ShapeDtypeStruct((m, d), F32), jax.ShapeDtypeStruct((m, dg), F32)],
        compiler_params=_params("arbitrary"),
        name="mixer_sample",
    )(x, norm_g, w_in, b_in, ln_g, ln_b, scale_row, bias_row, w_out)


def _mlp_kernel(*refs, final_norm, decode_pages, out_gain):
    refs = list(refs)
    if decode_pages:
        refs.pop(0)
    x_ref, g_ref, w1_ref, w2_ref = refs[:4]
    refs = refs[4:]
    fg_ref = refs.pop(0) if final_norm else None
    j = pl.program_id(1)
    r = pl.program_id(2)
    if decode_pages:
        lam_ref, q_ref, kn_ref, vn_ref, sg_ref = refs[:5]
        k_refs = refs[5:5 + decode_pages]
        v_refs = refs[5 + decode_pages:5 + 2 * decode_pages]
        o_ref, od_ref, hn_all, acc_all, qbd_all, spread_ref, m_all, l_all, dacc_all = refs[5 + 2 * decode_pages:]
        qbd_ref, m_ref, l_ref, dacc_ref = qbd_all.at[r], m_all.at[r], l_all.at[r], dacc_all.at[r]
        n_rows, d = qbd_ref.shape
        n_heads = n_rows // 2
        page = spread_ref.shape[0]
        dh = d // n_rows
    else:
        o_ref, hn_all, acc_all = refs
    hn_ref, acc_ref = hn_all.at[r], acc_all.at[r]

    @pl.when(j == 0)
    def _():
        x = x_ref[...]
        hn_ref[...] = (_rms_scale(x) * g_ref[...]).astype(BF16)
        acc_ref[...] = x
        if decode_pages:
            row = lax.broadcasted_iota(jnp.int32, (n_rows, d), 0)
            c = lax.broadcasted_iota(jnp.int32, (n_rows, d), 1)
            own = (c // dh) == jnp.where(row < n_heads, 2 * row, 2 * (row - n_heads) + 1)
            q_rows = jnp.where(own, q_ref[...].astype(F32), 0.0)
            qbd_ref[...] = q_rows.astype(BF16)
            pos = lax.broadcasted_iota(jnp.int32, spread_ref.shape, 0)
            slot = lax.broadcasted_iota(jnp.int32, spread_ref.shape, 1)
            spread_ref[...] = jnp.where(slot // n_heads == pos, 1.0, 0.0).astype(BF16)
            m_ref[...] = jnp.sum(q_rows * kn_ref[...], axis=-1, keepdims=True)
            l_ref[...] = jnp.ones(l_ref.shape, F32)
            vn = vn_ref[...]
            dacc_ref[...] = jnp.concatenate([vn, vn], axis=0)

    if decode_pages:
        kt = jnp.concatenate([k[...].astype(BF16) for k in k_refs], axis=1)
        s = jnp.dot(qbd_ref[...], kt, preferred_element_type=F32)
    a = jnp.dot(hn_ref[...], w1_ref[...], preferred_element_type=F32)
    if decode_pages:
        m_prev = m_ref[...]
        m_new = jnp.maximum(m_prev, jnp.max(s, axis=-1, keepdims=True))
        alpha = jnp.exp2(m_prev - m_new)
        pr = jnp.exp2(s - m_new)
        l_ref[...] = alpha * l_ref[...] + jnp.sum(pr, axis=-1, keepdims=True)
        m_ref[...] = m_new
        prb = pr.astype(BF16)
        p_rows = jnp.concatenate([prb[:, p * page:(p + 1) * page] for p in range(decode_pages)], axis=0)
        p_slots = jnp.dot(p_rows, spread_ref[...], preferred_element_type=F32)
    a = jnp.maximum(a, 0.0)
    a = (a * a).astype(BF16)
    acc_ref[...] += jnp.dot(a, w2_ref[...], preferred_element_type=F32)
    if decode_pages:
        rr = lax.broadcasted_iota(jnp.int32, p_slots.shape, 0)
        cc = lax.broadcasted_iota(jnp.int32, p_slots.shape, 1)
        p_own = jnp.where((cc % n_heads) == (rr % n_heads), p_slots, 0.0).astype(BF16)
        p_wide = jnp.concatenate([p_own[p * n_rows:(p + 1) * n_rows] for p in range(decode_pages)], axis=1)
        vp = jnp.concatenate([v[...].astype(BF16) for v in v_refs], axis=0)
        dacc_ref[...] = alpha * dacc_ref[...] + jnp.dot(p_wide, vp, preferred_element_type=F32)

    @pl.when(j == pl.num_programs(1) - 1)
    def _():
        y = acc_ref[...]
        if final_norm:
            y = _rms_scale(y) * fg_ref[...]
        o_ref[...] = y
        if decode_pages:
            o = dacc_ref[...] / l_ref[...]
            od_ref[...] = _diff_subln(o[:n_heads], o[n_heads:], lam_ref[0], sg_ref[...], out_gain
                                      ).astype(od_ref.dtype)


def _mlp(x, g, w1, w2, final_g=None, decode=None, *, layer, tm, tf, row_group=1):
    m, d = x.shape
    f = w1.shape[2]
    final_norm = final_g is not None
    n_tiles, n_hidden = m // tm, f // tf
    grid = (n_tiles // row_group, n_hidden, row_group)
    tile = lambda gi, r: gi * row_group + r
    x_idx = lambda gi, j, r, *_: (jnp.where(j == 0, tile(gi, r), tile(gi, row_group - 1)), 0)
    o_idx = lambda gi, j, r, *_: (jnp.where(j == n_hidden - 1, tile(gi, r), tile(gi, 0)), 0)
    in_specs = [pl.BlockSpec((tm, d), x_idx), _resident((1, d)),
                pl.BlockSpec((None, d, tf), lambda gi, j, r, *_: (layer, 0, j)),
                pl.BlockSpec((None, tf, d), lambda gi, j, r, *_: (layer, j, 0))]
    args = [x, g, w1, w2]
    out_specs = [pl.BlockSpec((tm, d), o_idx)]
    out_shape = [jax.ShapeDtypeStruct((m, d), F32)]
    scratch = [pltpu.VMEM((row_group, tm, d), BF16), pltpu.VMEM((row_group, tm, d), F32)]
    prefetch = []
    pages_per_step = 0
    out_gain = None
    if final_norm:
        in_specs.append(_resident((1, d)))
        args.append(final_g)
    if decode is not None:
        page_table, first_seq, lam, qb, k_new, v_new, subln_g, cache_k, cache_v, out_gain = decode
        n_seq, n_pages = page_table.shape
        n_phys, page, n_kh, dh = cache_k.shape
        n_vh, vd = cache_v.shape[2:]
        pages_per_step = n_pages // n_hidden
        assert first_seq + n_tiles <= n_seq and pages_per_step * n_hidden == n_pages
        ck = jnp.transpose(cache_k, (0, 2, 3, 1)).reshape(n_phys, n_kh * dh, page)
        cv = cache_v.reshape(n_phys, page * n_vh, vd)
        seq = lambda gi, r: first_seq + tile(gi, r)
        row = lambda: pl.BlockSpec((None, 1, d), lambda gi, j, r, pt: (seq(gi, r), 0, 0))

        def page_spec(p, rows, cols):
            return pl.BlockSpec(
                (None, rows, cols),
                lambda gi, j, r, pt: (pt[seq(gi, r) * n_pages + j * pages_per_step + p], 0, 0))

        prefetch = [page_table.reshape(-1)]
        in_specs += ([pl.BlockSpec(memory_space=pltpu.SMEM), row(), row(),
                      pl.BlockSpec((None, n_vh, vd), lambda gi, j, r, pt: (seq(gi, r), 0, 0)),
                      pl.BlockSpec((1, vd), lambda gi, j, r, pt: (0, 0))]
                     + [page_spec(p, n_kh * dh, page) for p in range(pages_per_step)]
                     + [page_spec(p, page * n_vh, vd) for p in range(pages_per_step)])
        args += [lam, qb.reshape(n_seq, 1, d), k_new.reshape(n_seq, 1, d), v_new.reshape(n_seq, n_vh, vd),
                 subln_g] + [ck] * pages_per_step + [cv] * pages_per_step
        out_specs.append(pl.BlockSpec((None, n_vh, vd), lambda gi, j, r, pt: o_idx(gi, j, r) + (0,)))
        out_shape.append(jax.ShapeDtypeStruct((n_tiles, n_vh, vd), BF16))
        scratch += [pltpu.VMEM((row_group, n_kh, d), BF16), pltpu.VMEM((page, page * n_vh), BF16),
                    pltpu.VMEM((row_group, n_kh, 1), F32), pltpu.VMEM((row_group, n_kh, 1), F32),
                    pltpu.VMEM((row_group, n_kh, vd), F32)]
    out = pl.pallas_call(
        functools.partial(_mlp_kernel, final_norm=final_norm, decode_pages=pages_per_step,
                          out_gain=out_gain),
        grid_spec=pltpu.PrefetchScalarGridSpec(
            num_scalar_prefetch=len(prefetch), grid=grid, in_specs=in_specs, out_specs=out_specs,
            scratch_shapes=scratch),
        out_shape=out_shape,
        compiler_params=_params("parallel", "arbitrary", "arbitrary"),
        name=("mlp_final" if final_norm else "mlp") + ("_decode" if decode is not None else ""),
    )(*prefetch, *args)
    if decode is None:
        return out[0]
    return out[0], out[1].reshape(n_tiles, n_vh * vd)


def _kvq_kernel(x_ref, gk_ref, gq_ref, wk_ref, wv_ref, wq_ref, k_ref, v_ref, kb_ref, vb_ref, qb_ref,
                *, q_scale):
    xs = _rms_scale(x_ref[...])
    nk = (xs * gk_ref[...]).astype(BF16)
    nq = (xs * gq_ref[...]).astype(BF16)
    k = jnp.dot(nk, wk_ref[...], preferred_element_type=F32)
    v = jnp.dot(nk, wv_ref[...], preferred_element_type=F32)
    q = jnp.dot(nq, wq_ref[...], preferred_element_type=F32) * q_scale
    k_ref[...] = k
    v_ref[...] = v
    kb_ref[...] = k.astype(BF16)
    vb_ref[...] = v.astype(BF16)
    qb_ref[...] = q.astype(BF16)


def _kvq(x, gk, gq, wk, wv, wq, *, tm, q_scale):
    m, d = x.shape
    row = lambda i: (i, 0)
    blk = lambda: pl.BlockSpec((tm, d), row)
    return pl.pallas_call(
        functools.partial(_kvq_kernel, q_scale=q_scale),
        grid=(m // tm,),
        in_specs=[blk(), _resident((1, d)), _resident((1, d)),
                  _resident(wk.shape), _resident(wv.shape), _resident(wq.shape)],
        out_specs=[blk(), blk(), blk(), blk(), blk()],
        out_shape=[jax.ShapeDtypeStruct((m, d), F32), jax.ShapeDtypeStruct((m, d), F32),
                   jax.ShapeDtypeStruct((m, d), BF16), jax.ShapeDtypeStruct((m, d), BF16),
                   jax.ShapeDtypeStruct((m, d), BF16)],
        compiler_params=_params("parallel"),
        name="kvq_proj",
    )(x, gk, gq, wk, wv, wq)


def _diff_subln(o1, o2, lam, g, out_gain):
    o = o1 - lam * o2
    return _rms_scale(o) * g * out_gain


ONES_ROWS = 16


def _attn_prompt_kernel(lam_ref, q_ref, k_ref, v_ref, g_ref, o_ref, vt_ref, qt_ref, s0_ref, s1_ref, m_ref,
                        acc_ref, *, out_gain):
    qi = pl.program_id(1)
    tq, dh2 = q_ref.shape
    tk = tq
    dh = dh2 // 2
    seq = k_ref.shape[0]
    t_chunk = 512

    @pl.when(qi == 0)
    def _():
        for c in range(seq // t_chunk):
            cols = slice(c * t_chunk, (c + 1) * t_chunk)
            vt_ref[0:dh2, cols] = v_ref[cols, :].astype(F32).T.astype(BF16)
        vt_ref[dh2:, :] = jnp.ones((ONES_ROWS, seq), BF16)

    q = q_ref[...].astype(F32)
    lane = lax.broadcasted_iota(jnp.int32, (1, dh2), 1)
    q_both = jnp.concatenate([jnp.where(lane < dh, q, 0.0), jnp.where(lane >= dh, q, 0.0)], axis=0)
    qt_ref[...] = q_both.T.astype(BF16)
    m_ref[...] = jnp.full(m_ref.shape, -jnp.inf, F32)
    acc_ref[...] = jnp.zeros(acc_ref.shape, F32)

    s_refs = (s0_ref, s1_ref)

    def scores(c, slot):
        k = k_ref[pl.ds(pl.multiple_of(c * tk, tk), tk), :]
        s_refs[slot][...] = jnp.dot(k, qt_ref[...], preferred_element_type=F32)

    def consume(c, slot, masked):
        s = s_refs[slot][...]
        if masked:
            k_pos = c * tk + lax.broadcasted_iota(jnp.int32, (tk, 2 * tq), 0)
            col = lax.broadcasted_iota(jnp.int32, (tk, 2 * tq), 1)
            q_pos = qi * tq + jnp.where(col < tq, col, col - tq)
            s = jnp.where(k_pos <= q_pos, s, -jnp.inf)
        m_prev = m_ref[...]
        m_new = jnp.maximum(m_prev, jnp.max(s, axis=0, keepdims=True))
        alpha = jnp.exp2(m_prev - m_new)
        p = jnp.exp2(s - m_new).astype(BF16)
        vt = vt_ref[:, pl.ds(pl.multiple_of(c * tk, tk), tk)]
        acc_ref[...] = alpha * acc_ref[...] + jnp.dot(vt, p, preferred_element_type=F32)
        m_ref[...] = m_new

    scores(0, 0)

    def pair(c, carry):
        scores(2 * c + 1, 1)
        consume(2 * c, 0, False)
        scores(2 * c + 2, 0)
        consume(2 * c + 1, 1, False)
        return carry

    lax.fori_loop(0, qi // 2, pair, 0)

    @pl.when(qi % 2 == 0)
    def _():
        consume(qi, 0, True)

    @pl.when(qi % 2 == 1)
    def _():
        scores(qi, 1)
        consume(qi - 1, 0, False)
        consume(qi, 1, True)

    acc = acc_ref[...]
    o_t = acc[:dh2] / acc[dh2:dh2 + 1]
    d_t = o_t[:, :tq] - lam_ref[0] * o_t[:, tq:]
    o_ref[...] = (_rms_scale(d_t.T) * g_ref[...] * out_gain).astype(o_ref.dtype)


def _attn_prompt(qb, kb, vb, lam, subln_g, *, tq, out_gain):
    s, d = qb.shape
    dh2 = d // N_HEADS_B
    return pl.pallas_call(
        functools.partial(_attn_prompt_kernel, out_gain=out_gain),
        grid=(N_HEADS_B, s // tq),
        in_specs=[pl.BlockSpec(memory_space=pltpu.SMEM),
                  pl.BlockSpec((tq, dh2), lambda h, i: (i, h)),
                  pl.BlockSpec((s, dh2), lambda h, i: (0, h)),
                  pl.BlockSpec((s, dh2), lambda h, i: (0, h)),
                  pl.BlockSpec((1, dh2), lambda h, i: (0, 0))],
        out_specs=pl.BlockSpec((tq, dh2), lambda h, i: (i, h)),
        out_shape=jax.ShapeDtypeStruct((s, d), BF16),
        scratch_shapes=[pltpu.VMEM((dh2 + ONES_ROWS, s), BF16), pltpu.VMEM((dh2, 2 * tq), BF16),
                        pltpu.VMEM((tq, 2 * tq), F32), pltpu.VMEM((tq, 2 * tq), F32),
                        pltpu.VMEM((1, 2 * tq), F32), pltpu.VMEM((dh2 + ONES_ROWS, 2 * tq), F32)],
        compiler_params=_params("parallel", "arbitrary"),
        name="attn_prompt",
    )(lam, qb, kb, vb, subln_g)


def _outproj_kernel(x_ref, o_ref, w_ref, y_ref):
    y_ref[...] = x_ref[...] + jnp.dot(o_ref[...], w_ref[...], preferred_element_type=F32)


def _outproj(x, o, w, *, tm):
    m, d = x.shape
    row = lambda i: (i, 0)
    return pl.pallas_call(
        _outproj_kernel,
        grid=(m // tm,),
        in_specs=[pl.BlockSpec((tm, d), row), pl.BlockSpec((tm, o.shape[1]), row), _resident(w.shape)],
        out_specs=pl.BlockSpec((tm, d), row),
        out_shape=jax.ShapeDtypeStruct((m, d), F32),
        compiler_params=_params("parallel"),
        name="attn_outproj",
    )(x, o, w)


def kernel(x_prompt, x_sample, cache_k, cache_v, page_table, a_norm_g, a_w_in, a_b_in, a_ln_g, a_ln_b,
           a_w_s, a_b_s, a_w_out, m_norm_g, m_w1, m_w2, kv_norm_g, w_k, w_v, b_norm_g, b_w_q, b_lq1,
           b_lk1, b_lq2, b_lk2, b_subln_g, b_w_o, final_norm_g):
    bp, sp, d = x_prompt.shape
    bs, ss, _ = x_sample.shape
    assert bp == 1 and ss == 1, "one prompt sequence and single-token sample rows"
    assert a_norm_g.shape[0] == 1 and b_norm_g.shape[0] == 1 and m_norm_g.shape[0] == 2
    n_kh, dh = cache_k.shape[2:]
    n_vh, vd = cache_v.shape[2:]
    dg = a_ln_g.shape[-1]
    gd = dg // N_GROUPS_A
    assert n_vh == N_HEADS_B and n_kh == 2 * N_HEADS_B and vd == 2 * dh == V7X_LANES and gd == V7X_LANES

    xp = x_prompt.reshape(sp, d)
    xs = x_sample.reshape(bs, d)
    vec = lambda a: a.reshape(1, -1).astype(F32)
    bf = lambda a: a.astype(BF16)

    mix_common = (vec(a_norm_g[0]), bf(a_w_in[0]), vec(a_b_in[0]), vec(a_ln_g[0]), vec(a_ln_b[0]))
    w_out = bf(a_w_out[0])
    per_group = lambda a: jnp.broadcast_to(a[..., None], a.shape + (gd,)).reshape(a.shape[:-1] + (dg,))
    bias_full = per_group(a_b_s[0].T)
    hp = _mixer_prompt(xp, *mix_common, a_w_s[0], bias_full, w_out, tm=256)
    scale_row = per_group(a_w_s[0][:, 0, 0]).reshape(1, dg)
    bias_row = bias_full[0:1]
    hs, chunk_v = _mixer_sample(xs, *mix_common, scale_row, bias_row, w_out)

    w1, w2 = bf(m_w1), bf(m_w2)
    hs = _mlp(hs, vec(m_norm_g[0]), w1, w2, layer=0, tm=bs, tf=2048)
    q_scale = dh ** -0.5 * LOG2_E
    proj = (vec(kv_norm_g), vec(b_norm_g[0]), bf(w_k), bf(w_v), bf(b_w_q[0]))
    ks, vs, _, _, qsb = _kvq(hs, *proj, tm=bs, q_scale=q_scale)

    lam_init = _lambda_init(1)
    f = lambda a: a.astype(F32)
    lam = (jnp.exp(jnp.sum(f(b_lq1[0]) * f(b_lk1[0]))) - jnp.exp(jnp.sum(f(b_lq2[0]) * f(b_lk2[0])))
           + lam_init).reshape(1).astype(F32)
    subln_g = vec(b_subln_g[0])
    out_gain = 1.0 - lam_init

    mlp_tm = sp // (bs // 2)
    mlp_tf = m_w1.shape[2] // (page_table.shape[1] // DECODE_PAGES_PER_STEP)
    decode = lambda first: (page_table, first, lam, qsb, ks, vs, subln_g, cache_k, cache_v, out_gain)

    hp, os_a = _mlp(hp, vec(m_norm_g[0]), w1, w2, None, decode(0), layer=0, tm=mlp_tm, tf=mlp_tf,
                    row_group=MLP_ROW_GROUP)

    kp, vp, kpb, vpb, qpb = _kvq(hp, *proj, tm=256, q_scale=q_scale)
    op = _attn_prompt(qpb, kpb, vpb, lam, subln_g, tq=512, out_gain=out_gain)
    w_o = bf(b_w_o[0])
    hp = _outproj(hp, op, w_o, tm=512)

    yp, os_b = _mlp(hp, vec(m_norm_g[1]), w1, w2, vec(final_norm_g), decode(bs // 2),
                    layer=1, tm=mlp_tm, tf=mlp_tf, row_group=MLP_ROW_GROUP)

    hs = _outproj(hs, jnp.concatenate([os_a, os_b], axis=0), w_o, tm=bs)
    ys = _mlp(hs, vec(m_norm_g[1]), w1, w2, vec(final_norm_g), layer=1, tm=bs, tf=2048)

    return (yp.reshape(bp, sp, d), ys.reshape(bs, ss, d),
            kp.reshape(bp, sp, n_kh, dh), vp.reshape(bp, sp, n_vh, vd),
            ks.reshape(bs, ss, n_kh, dh), vs.reshape(bs, ss, n_vh, vd),
            chunk_v.reshape(1, bs, ss, dg))
```

```python
import functools
import math

import jax
import jax.numpy as jnp
import numpy as np
from jax import lax
from jax.experimental import pallas as pl
from jax.experimental.pallas import tpu as pltpu

F32 = jnp.float32
BF16 = jnp.bfloat16

CHUNK = 128
N_GROUPS_A = 16
N_HEADS_B = 16
RMS_EPS = 1e-5
LN_EPS = 1e-5
SQRT_HALF = math.sqrt(0.5)
LOG2_E = math.log2(math.e)
DECODE_PAGES_PER_STEP = 4
MLP_ROW_GROUP = 2

V7X_LANES = 128
V7X_VMEM_LIMIT_BYTES = 56 * 1024 * 1024


def _lambda_init(layer):
    return 0.8 - 0.6 * math.exp(-0.3 * layer)


def _rms_scale(x):
    return x * lax.rsqrt(jnp.mean(x * x, axis=-1, keepdims=True) + RMS_EPS)


def _gelu_exact(z):
    return 0.5 * z * (1.0 + lax.erf(z * SQRT_HALF))


def _resident(shape):
    return pl.BlockSpec(shape, lambda *_: (0,) * len(shape), pipeline_mode=pl.Buffered(1))


def _params(*semantics):
    return pltpu.CompilerParams(dimension_semantics=semantics,
                                vmem_limit_bytes=V7X_VMEM_LIMIT_BYTES)


def _mixer_front(x, g_ref, win_ref, bin_ref, lng_ref, lnb_ref):
    d_gate = lng_ref.shape[-1]
    h = (_rms_scale(x) * g_ref[...]).astype(BF16)
    z = jnp.dot(h, win_ref[...], preferred_element_type=F32) + bin_ref[...]
    z = _gelu_exact(z)
    u = z[:, :d_gate]
    v = z[:, d_gate:]
    mu = jnp.mean(v, axis=-1, keepdims=True)
    vc = v - mu
    var = jnp.mean(vc * vc, axis=-1, keepdims=True)
    vn = vc * lax.rsqrt(var + LN_EPS) * lng_ref[...] + lnb_ref[...]
    return u, vn


def _mixer_prompt_kernel(x_ref, g_ref, win_ref, bin_ref, lng_ref, lnb_ref, ws_ref, bias_ref,
                         wout_ref, o_ref):
    x = x_ref[...]
    u, vn = _mixer_front(x, g_ref, win_ref, bin_ref, lng_ref, lnb_ref)
    vb = vn.astype(BF16)
    n_chunks = x.shape[0] // CHUNK
    gd = vn.shape[1] // N_GROUPS_A
    t_pos = lax.broadcasted_iota(jnp.int32, (CHUNK, CHUNK), 0)
    s_pos = lax.broadcasted_iota(jnp.int32, (CHUNK, CHUNK), 1)
    causal = s_pos <= t_pos
    blocks = [[None] * N_GROUPS_A for _ in range(n_chunks)]
    for g in range(N_GROUPS_A):
        w_masked = jnp.where(causal, ws_ref[g], 0.0).astype(BF16)
        rhs = jnp.concatenate(
            [vb[c * CHUNK:(c + 1) * CHUNK, g * gd:(g + 1) * gd] for c in range(n_chunks)], axis=1)
        mg = jnp.dot(w_masked, rhs, preferred_element_type=F32)
        for c in range(n_chunks):
            blocks[c][g] = mg[:, c * gd:(c + 1) * gd]
    bias = bias_ref[...]
    mixed = jnp.concatenate(
        [jnp.concatenate(blocks[c], axis=1) + bias for c in range(n_chunks)], axis=0)
    t = (u * mixed).astype(BF16)
    o_ref[...] = x + jnp.dot(t, wout_ref[...], preferred_element_type=F32)


def _mixer_sample_kernel(x_ref, g_ref, win_ref, bin_ref, lng_ref, lnb_ref, scale_ref, bias_ref,
                         wout_ref, o_ref, vn_ref):
    x = x_ref[...]
    u, vn = _mixer_front(x, g_ref, win_ref, bin_ref, lng_ref, lnb_ref)
    vn_ref[...] = vn
    mixed = vn * scale_ref[...] + bias_ref[...]
    t = (u * mixed).astype(BF16)
    o_ref[...] = x + jnp.dot(t, wout_ref[...], preferred_element_type=F32)


def _mixer_prompt(x, norm_g, w_in, b_in, ln_g, ln_b, w_s, bias_full, w_out, *, tm):
    m, d = x.shape
    dg = ln_g.shape[-1]
    row = lambda i: (i, 0)
    return pl.pallas_call(
        _mixer_prompt_kernel,
        grid=(m // tm,),
        in_specs=[pl.BlockSpec((tm, d), row), _resident((1, d)), _resident((d, 2 * dg)),
                  _resident((1, 2 * dg)), _resident((1, dg)), _resident((1, dg)),
                  _resident(w_s.shape), _resident((CHUNK, dg)), _resident((dg, d))],
        out_specs=pl.BlockSpec((tm, d), row),
        out_shape=jax.ShapeDtypeStruct((m, d), F32),
        compiler_params=_params("parallel"),
        name="mixer_prompt",
    )(x, norm_g, w_in, b_in, ln_g, ln_b, w_s, bias_full, w_out)


def _mixer_sample(x, norm_g, w_in, b_in, ln_g, ln_b, scale_row, bias_row, w_out):
    m, d = x.shape
    dg = ln_g.shape[-1]
    return pl.pallas_call(
        _mixer_sample_kernel,
        grid=(1,),
        in_specs=[_resident((m, d)), _resident((1, d)), _resident((d, 2 * dg)),
                  _resident((1, 2 * dg)), _resident((1, dg)), _resident((1, dg)),
                  _resident((1, dg)), _resident((1, dg)), _resident((dg, d))],
        out_specs=[pl.BlockSpec((m, d), lambda i: (0, 0)), pl.BlockSpec((m, dg), lambda i: (0, 0))],
        out_shape=[jax.ShapeDtypeStruct((m, d), F32), jax.ShapeDtypeStruct((m, dg), F32)],
        compiler_params=_params("arbitrary"),
        name="mixer_sample",
    )(x, norm_g, w_in, b_in, ln_g, ln_b, scale_row, bias_row, w_out)


def _mlp_kernel(*refs, final_norm, decode_pages, out_gain):
    refs = list(refs)
    if decode_pages:
        refs.pop(0)
    x_ref, g_ref, w1_ref, w2_ref = refs[:4]
    refs = refs[4:]
    fg_ref = refs.pop(0) if final_norm else None
    j = pl.program_id(1)
    r = pl.program_id(2)
    if decode_pages:
        lam_ref, q_ref, kn_ref, vn_ref, sg_ref = refs[:5]
        k_refs = refs[5:5 + decode_pages]
        v_refs = refs[5 + decode_pages:5 + 2 * decode_pages]
        o_ref, od_ref, hn_all, acc_all, qbd_all, spread_ref, m_all, l_all, dacc_all = refs[5 + 2 * decode_pages:]
        qbd_ref, m_ref, l_ref, dacc_ref = qbd_all.at[r], m_all.at[r], l_all.at[r], dacc_all.at[r]
        n_rows, d = qbd_ref.shape
        n_heads = n_rows // 2
        page = spread_ref.shape[0]
        dh = d // n_rows
    else:
        o_ref, hn_all, acc_all = refs
    hn_ref, acc_ref = hn_all.at[r], acc_all.at[r]

    @pl.when(j == 0)
    def _():
        x = x_ref[...]
        hn_ref[...] = (_rms_scale(x) * g_ref[...]).astype(BF16)
        acc_ref[...] = x
        if decode_pages:
            row = lax.broadcasted_iota(jnp.int32, (n_rows, d), 0)
            c = lax.broadcasted_iota(jnp.int32, (n_rows, d), 1)
            own = (c // dh) == jnp.where(row < n_heads, 2 * row, 2 * (row - n_heads) + 1)
            q_rows = jnp.where(own, q_ref[...].astype(F32), 0.0)
            qbd_ref[...] = q_rows.astype(BF16)
            pos = lax.broadcasted_iota(jnp.int32, spread_ref.shape, 0)
            slot = lax.broadcasted_iota(jnp.int32, spread_ref.shape, 1)
            spread_ref[...] = jnp.where(slot // n_heads == pos, 1.0, 0.0).astype(BF16)
            m_ref[...] = jnp.sum(q_rows * kn_ref[...], axis=-1, keepdims=True)
            l_ref[...] = jnp.ones(l_ref.shape, F32)
            vn = vn_ref[...]
            dacc_ref[...] = jnp.concatenate([vn, vn], axis=0)

    if decode_pages:
        kt = jnp.concatenate([k[...].astype(BF16) for k in k_refs], axis=1)
        s = jnp.dot(qbd_ref[...], kt, preferred_element_type=F32)
    a = jnp.dot(hn_ref[...], w1_ref[...], preferred_element_type=F32)
    if decode_pages:
        m_prev = m_ref[...]
        m_new = jnp.maximum(m_prev, jnp.max(s, axis=-1, keepdims=True))
        alpha = jnp.exp2(m_prev - m_new)
        pr = jnp.exp2(s - m_new)
        l_ref[...] = alpha * l_ref[...] + jnp.sum(pr, axis=-1, keepdims=True)
        m_ref[...] = m_new
        prb = pr.astype(BF16)
        p_rows = jnp.concatenate([prb[:, p * page:(p + 1) * page] for p in range(decode_pages)], axis=0)
        p_slots = jnp.dot(p_rows, spread_ref[...], preferred_element_type=F32)
    a = jnp.maximum(a, 0.0)
    a = (a * a).astype(BF16)
    acc_ref[...] += jnp.dot(a, w2_ref[...], preferred_element_type=F32)
    if decode_pages:
        rr = lax.broadcasted_iota(jnp.int32, p_slots.shape, 0)
        cc = lax.broadcasted_iota(jnp.int32, p_slots.shape, 1)
        p_own = jnp.where((cc % n_heads) == (rr % n_heads), p_slots, 0.0).astype(BF16)
        p_wide = jnp.concatenate([p_own[p * n_rows:(p + 1) * n_rows] for p in range(decode_pages)], axis=1)
        vp = jnp.concatenate([v[...].astype(BF16) for v in v_refs], axis=0)
        dacc_ref[...] = alpha * dacc_ref[...] + jnp.dot(p_wide, vp, preferred_element_type=F32)

    @pl.when(j == pl.num_programs(1) - 1)
    def _():
        y = acc_ref[...]
        if final_norm:
            y = _rms_scale(y) * fg_ref[...]
        o_ref[...] = y
        if decode_pages:
            o = dacc_ref[...] / l_ref[...]
            od_ref[...] = _diff_subln(o[:n_heads], o[n_heads:], lam_ref[0], sg_ref[...], out_gain
                                      ).astype(od_ref.dtype)


def _mlp(x, g, w1, w2, final_g=None, decode=None, *, layer, tm, tf, row_group=1):
    m, d = x.shape
    f = w1.shape[2]
    final_norm = final_g is not None
    n_tiles, n_hidden = m // tm, f // tf
    grid = (n_tiles // row_group, n_hidden, row_group)
    tile = lambda gi, r: gi * row_group + r
    x_idx = lambda gi, j, r, *_: (jnp.where(j == 0, tile(gi, r), tile(gi, row_group - 1)), 0)
    o_idx = lambda gi, j, r, *_: (jnp.where(j == n_hidden - 1, tile(gi, r), tile(gi, 0)), 0)
    in_specs = [pl.BlockSpec((tm, d), x_idx), _resident((1, d)),
                pl.BlockSpec((None, d, tf), lambda gi, j, r, *_: (layer, 0, j)),
                pl.BlockSpec((None, tf, d), lambda gi, j, r, *_: (layer, j, 0))]
    args = [x, g, w1, w2]
    out_specs = [pl.BlockSpec((tm, d), o_idx)]
    out_shape = [jax.ShapeDtypeStruct((m, d), F32)]
    scratch = [pltpu.VMEM((row_group, tm, d), BF16), pltpu.VMEM((row_group, tm, d), F32)]
    prefetch = []
    pages_per_step = 0
    out_gain = None
    if final_norm:
        in_specs.append(_resident((1, d)))
        args.append(final_g)
    if decode is not None:
        page_table, first_seq, lam, qb, k_new, v_new, subln_g, cache_k, cache_v, out_gain = decode
        n_seq, n_pages = page_table.shape
        n_phys, page, n_kh, dh = cache_k.shape
        n_vh, vd = cache_v.shape[2:]
        pages_per_step = n_pages // n_hidden
        assert first_seq + n_tiles <= n_seq and pages_per_step * n_hidden == n_pages
        ck = jnp.transpose(cache_k, (0, 2, 3, 1)).reshape(n_phys, n_kh * dh, page)
        cv = cache_v.reshape(n_phys, page * n_vh, vd)
        seq = lambda gi, r: first_seq + tile(gi, r)
        row = lambda: pl.BlockSpec((None, 1, d), lambda gi, j, r, pt: (seq(gi, r), 0, 0))

        def page_spec(p, rows, cols):
            return pl.BlockSpec(
                (None, rows, cols),
                lambda gi, j, r, pt: (pt[seq(gi, r) * n_pages + j * pages_per_step + p], 0, 0))

        prefetch = [page_table.reshape(-1)]
        in_specs += ([pl.BlockSpec(memory_space=pltpu.SMEM), row(), row(),
                      pl.BlockSpec((None, n_vh, vd), lambda gi, j, r, pt: (seq(gi, r), 0, 0)),
                      pl.BlockSpec((1, vd), lambda gi, j, r, pt: (0, 0))]
                     + [page_spec(p, n_kh * dh, page) for p in range(pages_per_step)]
                     + [page_spec(p, page * n_vh, vd) for p in range(pages_per_step)])
        args += [lam, qb.reshape(n_seq, 1, d), k_new.reshape(n_seq, 1, d), v_new.reshape(n_seq, n_vh, vd),
                 subln_g] + [ck] * pages_per_step + [cv] * pages_per_step
        out_specs.append(pl.BlockSpec((None, n_vh, vd), lambda gi, j, r, pt: o_idx(gi, j, r) + (0,)))
        out_shape.append(jax.ShapeDtypeStruct((n_tiles, n_vh, vd), BF16))
        scratch += [pltpu.VMEM((row_group, n_kh, d), BF16), pltpu.VMEM((page, page * n_vh), BF16),
                    pltpu.VMEM((row_group, n_kh, 1), F32), pltpu.VMEM((row_group, n_kh, 1), F32),
                    pltpu.VMEM((row_group, n_kh, vd), F32)]
    out = pl.pallas_call(
        functools.partial(_mlp_kernel, final_norm=final_norm, decode_pages=pages_per_step,
                          out_gain=out_gain),
        grid_spec=pltpu.PrefetchScalarGridSpec(
            num_scalar_prefetch=len(prefetch), grid=grid, in_specs=in_specs, out_specs=out_specs,
            scratch_shapes=scratch),
        out_shape=out_shape,
        compiler_params=_params("parallel", "arbitrary", "arbitrary"),
        name=("mlp_final" if final_norm else "mlp") + ("_decode" if decode is not None else ""),
    )(*prefetch, *args)
    if decode is None:
        return out[0]
    return out[0], out[1].reshape(n_tiles, n_vh * vd)


def _kvq_kernel(x_ref, gk_ref, gq_ref, wk_ref, wv_ref, wq_ref, k_ref, v_ref, kb_ref, vb_ref, qb_ref,
                *, q_scale):
    xs = _rms_scale(x_ref[...])
    nk = (xs * gk_ref[...]).astype(BF16)
    nq = (xs * gq_ref[...]).astype(BF16)
    k = jnp.dot(nk, wk_ref[...], preferred_element_type=F32)
    v = jnp.dot(nk, wv_ref[...], preferred_element_type=F32)
    q = jnp.dot(nq, wq_ref[...], preferred_element_type=F32) * q_scale
    k_ref[...] = k
    v_ref[...] = v
    kb_ref[...] = k.astype(BF16)
    vb_ref[...] = v.astype(BF16)
    qb_ref[...] = q.astype(BF16)


def _kvq(x, gk, gq, wk, wv, wq, *, tm, q_scale):
    m, d = x.shape
    row = lambda i: (i, 0)
    blk = lambda: pl.BlockSpec((tm, d), row)
    return pl.pallas_call(
        functools.partial(_kvq_kernel, q_scale=q_scale),
        grid=(m // tm,),
        in_specs=[blk(), _resident((1, d)), _resident((1, d)),
                  _resident(wk.shape), _resident(wv.shape), _resident(wq.shape)],
        out_specs=[blk(), blk(), blk(), blk(), blk()],
        out_shape=[jax.ShapeDtypeStruct((m, d), F32), jax.ShapeDtypeStruct((m, d), F32),
                   jax.ShapeDtypeStruct((m, d), BF16), jax.ShapeDtypeStruct((m, d), BF16),
                   jax.ShapeDtypeStruct((m, d), BF16)],
        compiler_params=_params("parallel"),
        name="kvq_proj",
    )(x, gk, gq, wk, wv, wq)


def _diff_subln(o1, o2, lam, g, out_gain):
    o = o1 - lam * o2
    return _rms_scale(o) * g * out_gain


ONES_ROWS = 16


def _attn_prompt_kernel(lam_ref, q_ref, qn_ref, k_ref, v_ref, g_ref, o_ref, vt_ref, qt_ref, s0_ref, s1_ref,
                        m_ref, acc_ref, *, out_gain):
    qi = pl.program_id(1)
    tq, dh2 = q_ref.shape
    tk = tq
    dh = dh2 // 2
    seq = k_ref.shape[0]
    t_chunk = 512
    s_refs = (s0_ref, s1_ref)

    def scores(c, slot):
        k = k_ref[pl.ds(pl.multiple_of(c * tk, tk), tk), :]
        s_refs[slot][...] = jnp.dot(k, qt_ref[...], preferred_element_type=F32)

    def start_tile(tile_ref):
        q = tile_ref[...].astype(F32)
        lane = lax.broadcasted_iota(jnp.int32, (1, dh2), 1)
        q_both = jnp.concatenate([jnp.where(lane < dh, q, 0.0), jnp.where(lane >= dh, q, 0.0)], axis=0)
        qt_ref[...] = q_both.T.astype(BF16)
        scores(0, 0)

    @pl.when(qi == 0)
    def _():
        for c in range(seq // t_chunk):
            cols = slice(c * t_chunk, (c + 1) * t_chunk)
            vt_ref[0:dh2, cols] = v_ref[cols, :].astype(F32).T.astype(BF16)
        vt_ref[dh2:, :] = jnp.ones((ONES_ROWS, seq), BF16)
        start_tile(q_ref)

    m_ref[...] = jnp.full(m_ref.shape, -jnp.inf, F32)
    acc_ref[...] = jnp.zeros(acc_ref.shape, F32)

    def consume(c, slot, masked):
        s = s_refs[slot][...]
        if masked:
            k_pos = c * tk + lax.broadcasted_iota(jnp.int32, (tk, 2 * tq), 0)
            col = lax.broadcasted_iota(jnp.int32, (tk, 2 * tq), 1)
            q_pos = qi * tq + jnp.where(col < tq, col, col - tq)
            s = jnp.where(k_pos <= q_pos, s, -jnp.inf)
        m_prev = m_ref[...]
        m_new = jnp.maximum(m_prev, jnp.max(s, axis=0, keepdims=True))
        alpha = jnp.exp2(m_prev - m_new)
        p = jnp.exp2(s - m_new).astype(BF16)
        vt = vt_ref[:, pl.ds(pl.multiple_of(c * tk, tk), tk)]
        acc_ref[...] = alpha * acc_ref[...] + jnp.dot(vt, p, preferred_element_type=F32)
        m_ref[...] = m_new

    def pair(c, carry):
        scores(2 * c + 1, 1)
        consume(2 * c, 0, False)
        scores(2 * c + 2, 0)
        consume(2 * c + 1, 1, False)
        return carry

    lax.fori_loop(0, qi // 2, pair, 0)

    def finish():
        start_tile(qn_ref)
        acc = acc_ref[...]
        o_t = acc[:dh2] / acc[dh2:dh2 + 1]
        d_t = o_t[:, :tq] - lam_ref[0] * o_t[:, tq:]
        o_ref[...] = (_rms_scale(d_t.T) * g_ref[...] * out_gain).astype(o_ref.dtype)

    @pl.when(qi % 2 == 0)
    def _():
        consume(qi, 0, True)
        finish()

    @pl.when(qi % 2 == 1)
    def _():
        scores(qi, 1)
        consume(qi - 1, 0, False)
        consume(qi, 1, True)
        finish()


def _attn_prompt(qb, kb, vb, lam, subln_g, *, tq, out_gain):
    s, d = qb.shape
    dh2 = d // N_HEADS_B
    return pl.pallas_call(
        functools.partial(_attn_prompt_kernel, out_gain=out_gain),
        grid=(N_HEADS_B, s // tq),
        in_specs=[pl.BlockSpec(memory_space=pltpu.SMEM),
                  pl.BlockSpec((tq, dh2), lambda h, i: (i, h)),
                  pl.BlockSpec((tq, dh2), lambda h, i: (jnp.minimum(i + 1, s // tq - 1), h)),
                  pl.BlockSpec((s, dh2), lambda h, i: (0, h)),
                  pl.BlockSpec((s, dh2), lambda h, i: (0, h)),
                  pl.BlockSpec((1, dh2), lambda h, i: (0, 0))],
        out_specs=pl.BlockSpec((tq, dh2), lambda h, i: (i, h)),
        out_shape=jax.ShapeDtypeStruct((s, d), BF16),
        scratch_shapes=[pltpu.VMEM((dh2 + ONES_ROWS, s), BF16), pltpu.VMEM((dh2, 2 * tq), BF16),
                        pltpu.VMEM((tq, 2 * tq), F32), pltpu.VMEM((tq, 2 * tq), F32),
                        pltpu.VMEM((1, 2 * tq), F32), pltpu.VMEM((dh2 + ONES_ROWS, 2 * tq), F32)],
        compiler_params=_params("parallel", "arbitrary"),
        name="attn_prompt",
    )(lam, qb, qb, kb, vb, subln_g)


def _outproj_kernel(x_ref, o_ref, w_ref, y_ref):
    y_ref[...] = x_ref[...] + jnp.dot(o_ref[...], w_ref[...], preferred_element_type=F32)


def _outproj(x, o, w, *, tm):
    m, d = x.shape
    row = lambda i: (i, 0)
    return pl.pallas_call(
        _outproj_kernel,
        grid=(m // tm,),
        in_specs=[pl.BlockSpec((tm, d), row), pl.BlockSpec((tm, o.shape[1]), row), _resident(w.shape)],
        out_specs=pl.BlockSpec((tm, d), row),
        out_shape=jax.ShapeDtypeStruct((m, d), F32),
        compiler_params=_params("parallel"),
        name="attn_outproj",
    )(x, o, w)


def kernel(x_prompt, x_sample, cache_k, cache_v, page_table, a_norm_g, a_w_in, a_b_in, a_ln_g, a_ln_b,
           a_w_s, a_b_s, a_w_out, m_norm_g, m_w1, m_w2, kv_norm_g, w_k, w_v, b_norm_g, b_w_q, b_lq1,
           b_lk1, b_lq2, b_lk2, b_subln_g, b_w_o, final_norm_g):
    bp, sp, d = x_prompt.shape
    bs, ss, _ = x_sample.shape
    assert bp == 1 and ss == 1, "one prompt sequence and single-token sample rows"
    assert a_norm_g.shape[0] == 1 and b_norm_g.shape[0] == 1 and m_norm_g.shape[0] == 2
    n_kh, dh = cache_k.shape[2:]
    n_vh, vd = cache_v.shape[2:]
    dg = a_ln_g.shape[-1]
    gd = dg // N_GROUPS_A
    assert n_vh == N_HEADS_B and n_kh == 2 * N_HEADS_B and vd == 2 * dh == V7X_LANES and gd == V7X_LANES

    xp = x_prompt.reshape(sp, d)
    xs = x_sample.reshape(bs, d)
    vec = lambda a: a.reshape(1, -1).astype(F32)
    bf = lambda a: a.astype(BF16)

    mix_common = (vec(a_norm_g[0]), bf(a_w_in[0]), vec(a_b_in[0]), vec(a_ln_g[0]), vec(a_ln_b[0]))
    w_out = bf(a_w_out[0])
    per_group = lambda a: jnp.broadcast_to(a[..., None], a.shape + (gd,)).reshape(a.shape[:-1] + (dg,))
    bias_full = per_group(a_b_s[0].T)
    hp = _mixer_prompt(xp, *mix_common, a_w_s[0], bias_full, w_out, tm=256)
    scale_row = per_group(a_w_s[0][:, 0, 0]).reshape(1, dg)
    bias_row = bias_full[0:1]
    hs, chunk_v = _mixer_sample(xs, *mix_common, scale_row, bias_row, w_out)

    w1, w2 = bf(m_w1), bf(m_w2)
    hs = _mlp(hs, vec(m_norm_g[0]), w1, w2, layer=0, tm=bs, tf=2048)
    q_scale = dh ** -0.5 * LOG2_E
    proj = (vec(kv_norm_g), vec(b_norm_g[0]), bf(w_k), bf(w_v), bf(b_w_q[0]))
    ks, vs, _, _, qsb = _kvq(hs, *proj, tm=bs, q_scale=q_scale)

    lam_init = _lambda_init(1)
    f = lambda a: a.astype(F32)
    lam = (jnp.exp(jnp.sum(f(b_lq1[0]) * f(b_lk1[0]))) - jnp.exp(jnp.sum(f(b_lq2[0]) * f(b_lk2[0])))
           + lam_init).reshape(1).astype(F32)
    subln_g = vec(b_subln_g[0])
    out_gain = 1.0 - lam_init

    mlp_tm = sp // (bs // 2)
    mlp_tf = m_w1.shape[2] // (page_table.shape[1] // DECODE_PAGES_PER_STEP)
    decode = lambda first: (page_table, first, lam, qsb, ks, vs, subln_g, cache_k, cache_v, out_gain)

    hp, os_a = _mlp(hp, vec(m_norm_g[0]), w1, w2, None, decode(0), layer=0, tm=mlp_tm, tf=mlp_tf,
                    row_group=MLP_ROW_GROUP)

    kp, vp, kpb, vpb, qpb = _kvq(hp, *proj, tm=256, q_scale=q_scale)
    op = _attn_prompt(qpb, kpb, vpb, lam, subln_g, tq=512, out_gain=out_gain)
    w_o = bf(b_w_o[0])
    hp = _outproj(hp, op, w_o, tm=512)

    yp, os_b = _mlp(hp, vec(m_norm_g[1]), w1, w2, vec(final_norm_g), decode(bs // 2),
                    layer=1, tm=mlp_tm, tf=mlp_tf, row_group=MLP_ROW_GROUP)

    hs = _outproj(hs, jnp.concatenate([os_a, os_b], axis=0), w_o, tm=bs)
    ys = _mlp(hs, vec(m_norm_g[1]), w1, w2, vec(final_norm_g), layer=1, tm=bs, tf=2048)

    return (yp.reshape(bp, sp, d), ys.reshape(bs, ss, d),
            kp.reshape(bp, sp, n_kh, dh), vp.reshape(bp, sp, n_vh, vd),
            ks.reshape(bs, ss, n_kh, dh), vs.reshape(bs, ss, n_vh, vd),
            chunk_v.reshape(1, bs, ss, dg))
```

```python
import functools
import math

import jax
import jax.numpy as jnp
import numpy as np
from jax import lax
from jax.experimental import pallas as pl
from jax.experimental.pallas import tpu as pltpu

F32 = jnp.float32
BF16 = jnp.bfloat16

CHUNK = 128
N_GROUPS_A = 16
N_HEADS_B = 16
RMS_EPS = 1e-5
LN_EPS = 1e-5
SQRT_HALF = math.sqrt(0.5)
LOG2_E = math.log2(math.e)
DECODE_PAGES_PER_STEP = 4
MLP_ROW_GROUP = 2

V7X_LANES = 128
V7X_VMEM_LIMIT_BYTES = 56 * 1024 * 1024


def _lambda_init(layer):
    return 0.8 - 0.6 * math.exp(-0.3 * layer)


def _rms_scale(x):
    return x * lax.rsqrt(jnp.mean(x * x, axis=-1, keepdims=True) + RMS_EPS)


def _gelu_exact(z):
    return 0.5 * z * (1.0 + lax.erf(z * SQRT_HALF))


def _resident(shape):
    return pl.BlockSpec(shape, lambda *_: (0,) * len(shape), pipeline_mode=pl.Buffered(1))


def _params(*semantics):
    return pltpu.CompilerParams(dimension_semantics=semantics,
                                vmem_limit_bytes=V7X_VMEM_LIMIT_BYTES)


def _mixer_front(x, g_ref, win_ref, bin_ref, lng_ref, lnb_ref):
    d_gate = lng_ref.shape[-1]
    h = (_rms_scale(x) * g_ref[...]).astype(BF16)
    z = jnp.dot(h, win_ref[...], preferred_element_type=F32) + bin_ref[...]
    z = _gelu_exact(z)
    u = z[:, :d_gate]
    v = z[:, d_gate:]
    mu = jnp.mean(v, axis=-1, keepdims=True)
    vc = v - mu
    var = jnp.mean(vc * vc, axis=-1, keepdims=True)
    vn = vc * lax.rsqrt(var + LN_EPS) * lng_ref[...] + lnb_ref[...]
    return u, vn


def _mixer_prompt_kernel(x_ref, g_ref, win_ref, bin_ref, lng_ref, lnb_ref, ws_ref, bias_ref,
                         wout_ref, o_ref):
    x = x_ref[...]
    u, vn = _mixer_front(x, g_ref, win_ref, bin_ref, lng_ref, lnb_ref)
    vb = vn.astype(BF16)
    n_chunks = x.shape[0] // CHUNK
    gd = vn.shape[1] // N_GROUPS_A
    t_pos = lax.broadcasted_iota(jnp.int32, (CHUNK, CHUNK), 0)
    s_pos = lax.broadcasted_iota(jnp.int32, (CHUNK, CHUNK), 1)
    causal = s_pos <= t_pos
    blocks = [[None] * N_GROUPS_A for _ in range(n_chunks)]
    for g in range(N_GROUPS_A):
        w_masked = jnp.where(causal, ws_ref[g], 0.0).astype(BF16)
        rhs = jnp.concatenate(
            [vb[c * CHUNK:(c + 1) * CHUNK, g * gd:(g + 1) * gd] for c in range(n_chunks)], axis=1)
        mg = jnp.dot(w_masked, rhs, preferred_element_type=F32)
        for c in range(n_chunks):
            blocks[c][g] = mg[:, c * gd:(c + 1) * gd]
    bias = bias_ref[...]
    mixed = jnp.concatenate(
        [jnp.concatenate(blocks[c], axis=1) + bias for c in range(n_chunks)], axis=0)
    t = (u * mixed).astype(BF16)
    o_ref[...] = x + jnp.dot(t, wout_ref[...], preferred_element_type=F32)


def _mixer_sample_kernel(x_ref, g_ref, win_ref, bin_ref, lng_ref, lnb_ref, scale_ref, bias_ref,
                         wout_ref, o_ref, vn_ref):
    x = x_ref[...]
    u, vn = _mixer_front(x, g_ref, win_ref, bin_ref, lng_ref, lnb_ref)
    vn_ref[...] = vn
    mixed = vn * scale_ref[...] + bias_ref[...]
    t = (u * mixed).astype(BF16)
    o_ref[...] = x + jnp.dot(t, wout_ref[...], preferred_element_type=F32)


def _mixer_prompt(x, norm_g, w_in, b_in, ln_g, ln_b, w_s, bias_full, w_out, *, tm):
    m, d = x.shape
    dg = ln_g.shape[-1]
    row = lambda i: (i, 0)
    return pl.pallas_call(
        _mixer_prompt_kernel,
        grid=(m // tm,),
        in_specs=[pl.BlockSpec((tm, d), row), _resident((1, d)), _resident((d, 2 * dg)),
                  _resident((1, 2 * dg)), _resident((1, dg)), _resident((1, dg)),
                  _resident(w_s.shape), _resident((CHUNK, dg)), _resident((dg, d))],
        out_specs=pl.BlockSpec((tm, d), row),
        out_shape=jax.ShapeDtypeStruct((m, d), F32),
        compiler_params=_params("parallel"),
        name="mixer_prompt",
    )(x, norm_g, w_in, b_in, ln_g, ln_b, w_s, bias_full, w_out)


def _mixer_sample(x, norm_g, w_in, b_in, ln_g, ln_b, scale_row, bias_row, w_out):
    m, d = x.shape
    dg = ln_g.shape[-1]
    return pl.pallas_call(
        _mixer_sample_kernel,
        grid=(1,),
        in_specs=[_resident((m, d)), _resident((1, d)), _resident((d, 2 * dg)),
                  _resident((1, 2 * dg)), _resident((1, dg)), _resident((1, dg)),
                  _resident((1, dg)), _resident((1, dg)), _resident((dg, d))],
        out_specs=[pl.BlockSpec((m, d), lambda i: (0, 0)), pl.BlockSpec((m, dg), lambda i: (0, 0))],
        out_shape=[jax.ShapeDtypeStruct((m, d), F32), jax.ShapeDtypeStruct((m, dg), F32)],
        compiler_params=_params("arbitrary"),
        name="mixer_sample",
    )(x, norm_g, w_in, b_in, ln_g, ln_b, scale_row, bias_row, w_out)


def _mlp_kernel(*refs, final_norm, decode_pages, out_gain):
    refs = list(refs)
    if decode_pages:
        refs.pop(0)
    x_ref, g_ref, w1_ref, w2_ref = refs[:4]
    refs = refs[4:]
    fg_ref = refs.pop(0) if final_norm else None
    j = pl.program_id(1)
    r = pl.program_id(2)
    if decode_pages:
        lam_ref, q_ref, kn_ref, vn_ref, sg_ref = refs[:5]
        k_refs = refs[5:5 + decode_pages]
        v_refs = refs[5 + decode_pages:5 + 2 * decode_pages]
        o_ref, od_ref, hn_all, acc_all, qbd_all, spread_ref, m_all, l_all, dacc_all = refs[5 + 2 * decode_pages:]
        qbd_ref, m_ref, l_ref, dacc_ref = qbd_all.at[r], m_all.at[r], l_all.at[r], dacc_all.at[r]
        n_rows, d = qbd_ref.shape
        n_heads = n_rows // 2
        page = spread_ref.shape[0]
        dh = d // n_rows
    else:
        o_ref, hn_all, acc_all = refs
    hn_ref, acc_ref = hn_all.at[r], acc_all.at[r]

    @pl.when(j == 0)
    def _():
        x = x_ref[...]
        hn_ref[...] = (_rms_scale(x) * g_ref[...]).astype(BF16)
        acc_ref[...] = x
        if decode_pages:
            row = lax.broadcasted_iota(jnp.int32, (n_rows, d), 0)
            c = lax.broadcasted_iota(jnp.int32, (n_rows, d), 1)
            own = (c // dh) == jnp.where(row < n_heads, 2 * row, 2 * (row - n_heads) + 1)
            q_rows = jnp.where(own, q_ref[...].astype(F32), 0.0)
            qbd_ref[...] = q_rows.astype(BF16)
            pos = lax.broadcasted_iota(jnp.int32, spread_ref.shape, 0)
            slot = lax.broadcasted_iota(jnp.int32, spread_ref.shape, 1)
            spread_ref[...] = jnp.where(slot // n_heads == pos, 1.0, 0.0).astype(BF16)
            m_ref[...] = jnp.sum(q_rows * kn_ref[...], axis=-1, keepdims=True)
            l_ref[...] = jnp.ones(l_ref.shape, F32)
            vn = vn_ref[...]
            dacc_ref[...] = jnp.concatenate([vn, vn], axis=0)

    if decode_pages:
        kt = jnp.concatenate([k[...].astype(BF16) for k in k_refs], axis=1)
        s = jnp.dot(qbd_ref[...], kt, preferred_element_type=F32)
    a = jnp.dot(hn_ref[...], w1_ref[...], preferred_element_type=F32)
    if decode_pages:
        m_prev = m_ref[...]
        m_new = jnp.maximum(m_prev, jnp.max(s, axis=-1, keepdims=True))
        alpha = jnp.exp2(m_prev - m_new)
        pr = jnp.exp2(s - m_new)
        l_ref[...] = alpha * l_ref[...] + jnp.sum(pr, axis=-1, keepdims=True)
        m_ref[...] = m_new
        prb = pr.astype(BF16)
        p_rows = jnp.concatenate([prb[:, p * page:(p + 1) * page] for p in range(decode_pages)], axis=0)
        p_slots = jnp.dot(p_rows, spread_ref[...], preferred_element_type=F32)
    a = jnp.maximum(a, 0.0)
    a = (a * a).astype(BF16)
    acc_ref[...] += jnp.dot(a, w2_ref[...], preferred_element_type=F32)
    if decode_pages:
        rr = lax.broadcasted_iota(jnp.int32, p_slots.shape, 0)
        cc = lax.broadcasted_iota(jnp.int32, p_slots.shape, 1)
        p_own = jnp.where((cc % n_heads) == (rr % n_heads), p_slots, 0.0).astype(BF16)
        p_wide = jnp.concatenate([p_own[p * n_rows:(p + 1) * n_rows] for p in range(decode_pages)], axis=1)
        vp = jnp.concatenate([v[...].astype(BF16) for v in v_refs], axis=0)
        dacc_ref[...] = alpha * dacc_ref[...] + jnp.dot(p_wide, vp, preferred_element_type=F32)

    @pl.when(j == pl.num_programs(1) - 1)
    def _():
        y = acc_ref[...]
        if final_norm:
            y = _rms_scale(y) * fg_ref[...]
        o_ref[...] = y
        if decode_pages:
            o = dacc_ref[...] / l_ref[...]
            od_ref[...] = _diff_subln(o[:n_heads], o[n_heads:], lam_ref[0], sg_ref[...], out_gain
                                      ).astype(od_ref.dtype)


def _mlp(x, g, w1, w2, final_g=None, decode=None, *, layer, tm, tf, row_group=1):
    m, d = x.shape
    f = w1.shape[2]
    final_norm = final_g is not None
    n_tiles, n_hidden = m // tm, f // tf
    grid = (n_tiles // row_group, n_hidden, row_group)
    tile = lambda gi, r: gi * row_group + r
    x_idx = lambda gi, j, r, *_: (jnp.where(j == 0, tile(gi, r), tile(gi, row_group - 1)), 0)
    o_idx = lambda gi, j, r, *_: (jnp.where(j == n_hidden - 1, tile(gi, r), tile(gi, 0)), 0)
    in_specs = [pl.BlockSpec((tm, d), x_idx), _resident((1, d)),
                pl.BlockSpec((None, d, tf), lambda gi, j, r, *_: (layer, 0, j)),
                pl.BlockSpec((None, tf, d), lambda gi, j, r, *_: (layer, j, 0))]
    args = [x, g, w1, w2]
    out_specs = [pl.BlockSpec((tm, d), o_idx)]
    out_shape = [jax.ShapeDtypeStruct((m, d), F32)]
    scratch = [pltpu.VMEM((row_group, tm, d), BF16), pltpu.VMEM((row_group, tm, d), F32)]
    prefetch = []
    pages_per_step = 0
    out_gain = None
    if final_norm:
        in_specs.append(_resident((1, d)))
        args.append(final_g)
    if decode is not None:
        page_table, first_seq, lam, qb, k_new, v_new, subln_g, cache_k, cache_v, out_gain = decode
        n_seq, n_pages = page_table.shape
        n_phys, page, n_kh, dh = cache_k.shape
        n_vh, vd = cache_v.shape[2:]
        pages_per_step = n_pages // n_hidden
        assert first_seq + n_tiles <= n_seq and pages_per_step * n_hidden == n_pages
        ck = jnp.transpose(cache_k, (0, 2, 3, 1)).reshape(n_phys, n_kh * dh, page)
        cv = cache_v.reshape(n_phys, page * n_vh, vd)
        seq = lambda gi, r: first_seq + tile(gi, r)
        row = lambda: pl.BlockSpec((None, 1, d), lambda gi, j, r, pt: (seq(gi, r), 0, 0))

        def page_spec(p, rows, cols):
            return pl.BlockSpec(
                (None, rows, cols),
                lambda gi, j, r, pt: (pt[seq(gi, r) * n_pages + j * pages_per_step + p], 0, 0))

        prefetch = [page_table.reshape(-1)]
        in_specs += ([pl.BlockSpec(memory_space=pltpu.SMEM), row(), row(),
                      pl.BlockSpec((None, n_vh, vd), lambda gi, j, r, pt: (seq(gi, r), 0, 0)),
                      pl.BlockSpec((1, vd), lambda gi, j, r, pt: (0, 0))]
                     + [page_spec(p, n_kh * dh, page) for p in range(pages_per_step)]
                     + [page_spec(p, page * n_vh, vd) for p in range(pages_per_step)])
        args += [lam, qb.reshape(n_seq, 1, d), k_new.reshape(n_seq, 1, d), v_new.reshape(n_seq, n_vh, vd),
                 subln_g] + [ck] * pages_per_step + [cv] * pages_per_step
        out_specs.append(pl.BlockSpec((None, n_vh, vd), lambda gi, j, r, pt: o_idx(gi, j, r) + (0,)))
        out_shape.append(jax.ShapeDtypeStruct((n_tiles, n_vh, vd), BF16))
        scratch += [pltpu.VMEM((row_group, n_kh, d), BF16), pltpu.VMEM((page, page * n_vh), BF16),
                    pltpu.VMEM((row_group, n_kh, 1), F32), pltpu.VMEM((row_group, n_kh, 1), F32),
                    pltpu.VMEM((row_group, n_kh, vd), F32)]
    out = pl.pallas_call(
        functools.partial(_mlp_kernel, final_norm=final_norm, decode_pages=pages_per_step,
                          out_gain=out_gain),
        grid_spec=pltpu.PrefetchScalarGridSpec(
            num_scalar_prefetch=len(prefetch), grid=grid, in_specs=in_specs, out_specs=out_specs,
            scratch_shapes=scratch),
        out_shape=out_shape,
        compiler_params=_params("parallel", "arbitrary", "arbitrary"),
        name=("mlp_final" if final_norm else "mlp") + ("_decode" if decode is not None else ""),
    )(*prefetch, *args)
    if decode is None:
        return out[0]
    return out[0], out[1].reshape(n_tiles, n_vh * vd)


def _kvq_kernel(x_ref, gk_ref, gq_ref, wk_ref, wv_ref, wq_ref, k_ref, v_ref, kb_ref, vb_ref, qb_ref,
                *, q_scale):
    xs = _rms_scale(x_ref[...])
    nk = (xs * gk_ref[...]).astype(BF16)
    nq = (xs * gq_ref[...]).astype(BF16)
    k = jnp.dot(nk, wk_ref[...], preferred_element_type=F32)
    v = jnp.dot(nk, wv_ref[...], preferred_element_type=F32)
    q = jnp.dot(nq, wq_ref[...], preferred_element_type=F32) * q_scale
    k_ref[...] = k
    v_ref[...] = v
    kb_ref[...] = k.astype(BF16)
    vb_ref[...] = v.astype(BF16)
    qb_ref[...] = q.astype(BF16)


def _kvq(x, gk, gq, wk, wv, wq, *, tm, q_scale):
    m, d = x.shape
    row = lambda i: (i, 0)
    blk = lambda: pl.BlockSpec((tm, d), row)
    return pl.pallas_call(
        functools.partial(_kvq_kernel, q_scale=q_scale),
        grid=(m // tm,),
        in_specs=[blk(), _resident((1, d)), _resident((1, d)),
                  _resident(wk.shape), _resident(wv.shape), _resident(wq.shape)],
        out_specs=[blk(), blk(), blk(), blk(), blk()],
        out_shape=[jax.ShapeDtypeStruct((m, d), F32), jax.ShapeDtypeStruct((m, d), F32),
                   jax.ShapeDtypeStruct((m, d), BF16), jax.ShapeDtypeStruct((m, d), BF16),
                   jax.ShapeDtypeStruct((m, d), BF16)],
        compiler_params=_params("parallel"),
        name="kvq_proj",
    )(x, gk, gq, wk, wv, wq)


def _diff_subln(o1, o2, lam, g, out_gain):
    o = o1 - lam * o2
    return _rms_scale(o) * g * out_gain


ONES_ROWS = 16


def _attn_prompt_kernel(lam_ref, q_ref, qn_ref, k_ref, v_ref, g_ref, o_ref, vt_ref, qt_ref, s0_ref, s1_ref,
                        m_ref, acc_ref, *, out_gain):
    qi = pl.program_id(1)
    tq, dh2 = q_ref.shape
    tk = tq
    dh = dh2 // 2
    seq = k_ref.shape[0]
    t_chunk = tq
    s_refs = (s0_ref, s1_ref)

    def scores(c, slot):
        k = k_ref[pl.ds(pl.multiple_of(c * tk, tk), tk), :]
        s_refs[slot][...] = jnp.dot(k, qt_ref[...], preferred_element_type=F32)

    def start_tile(tile_ref):
        q = tile_ref[...].astype(F32)
        lane = lax.broadcasted_iota(jnp.int32, (1, dh2), 1)
        q_both = jnp.concatenate([jnp.where(lane < dh, q, 0.0), jnp.where(lane >= dh, q, 0.0)], axis=0)
        qt_ref[...] = q_both.T.astype(BF16)
        scores(0, 0)

    @pl.when(qi == 0)
    def _():
        for c in range(seq // t_chunk):
            cols = slice(c * t_chunk, (c + 1) * t_chunk)
            vt_ref[0:dh2, cols] = v_ref[cols, :].astype(F32).T.astype(BF16)
        vt_ref[dh2:, :] = jnp.ones((ONES_ROWS, seq), BF16)
        start_tile(q_ref)

    m_ref[...] = jnp.full(m_ref.shape, -jnp.inf, F32)
    acc_ref[...] = jnp.zeros(acc_ref.shape, F32)

    def consume(c, slot, masked):
        s = s_refs[slot][...]
        if masked:
            k_pos = c * tk + lax.broadcasted_iota(jnp.int32, (tk, 2 * tq), 0)
            col = lax.broadcasted_iota(jnp.int32, (tk, 2 * tq), 1)
            q_pos = qi * tq + jnp.where(col < tq, col, col - tq)
            s = jnp.where(k_pos <= q_pos, s, -jnp.inf)
        m_prev = m_ref[...]
        m_new = jnp.maximum(m_prev, jnp.max(s, axis=0, keepdims=True))
        alpha = jnp.exp2(m_prev - m_new)
        p = jnp.exp2(s - m_new).astype(BF16)
        vt = vt_ref[:, pl.ds(pl.multiple_of(c * tk, tk), tk)]
        acc_ref[...] = alpha * acc_ref[...] + jnp.dot(vt, p, preferred_element_type=F32)
        m_ref[...] = m_new

    def pair(c, carry):
        scores(2 * c + 1, 1)
        consume(2 * c, 0, False)
        scores(2 * c + 2, 0)
        consume(2 * c + 1, 1, False)
        return carry

    def two_pairs(c, carry):
        return pair(2 * c + 1, pair(2 * c, carry))

    lax.fori_loop(0, qi // 4, two_pairs, 0)
    lax.fori_loop(2 * (qi // 4), qi // 2, pair, 0)

    def finish():
        start_tile(qn_ref)
        acc = acc_ref[...]
        o_t = acc[:dh2] / acc[dh2:dh2 + 1]
        d_t = o_t[:, :tq] - lam_ref[0] * o_t[:, tq:]
        o_ref[...] = (_rms_scale(d_t.T) * g_ref[...] * out_gain).astype(o_ref.dtype)

    @pl.when(qi % 2 == 0)
    def _():
        consume(qi, 0, True)
        finish()

    @pl.when(qi % 2 == 1)
    def _():
        scores(qi, 1)
        consume(qi - 1, 0, False)
        consume(qi, 1, True)
        finish()


def _attn_prompt(qb, kb, vb, lam, subln_g, *, tq, out_gain):
    s, d = qb.shape
    dh2 = d // N_HEADS_B
    return pl.pallas_call(
        functools.partial(_attn_prompt_kernel, out_gain=out_gain),
        grid=(N_HEADS_B, s // tq),
        in_specs=[pl.BlockSpec(memory_space=pltpu.SMEM),
                  pl.BlockSpec((tq, dh2), lambda h, i: (i, h)),
                  pl.BlockSpec((tq, dh2), lambda h, i: (jnp.minimum(i + 1, s // tq - 1), h)),
                  pl.BlockSpec((s, dh2), lambda h, i: (0, h)),
                  pl.BlockSpec((s, dh2), lambda h, i: (0, h)),
                  pl.BlockSpec((1, dh2), lambda h, i: (0, 0))],
        out_specs=pl.BlockSpec((tq, dh2), lambda h, i: (i, h)),
        out_shape=jax.ShapeDtypeStruct((s, d), BF16),
        scratch_shapes=[pltpu.VMEM((dh2 + ONES_ROWS, s), BF16), pltpu.VMEM((dh2, 2 * tq), BF16),
                        pltpu.VMEM((tq, 2 * tq), F32), pltpu.VMEM((tq, 2 * tq), F32),
                        pltpu.VMEM((1, 2 * tq), F32), pltpu.VMEM((dh2 + ONES_ROWS, 2 * tq), F32)],
        compiler_params=_params("parallel", "arbitrary"),
        name="attn_prompt",
    )(lam, qb, qb, kb, vb, subln_g)


def _outproj_kernel(x_ref, o_ref, w_ref, y_ref):
    y_ref[...] = x_ref[...] + jnp.dot(o_ref[...], w_ref[...], preferred_element_type=F32)


def _outproj(x, o, w, *, tm):
    m, d = x.shape
    row = lambda i: (i, 0)
    return pl.pallas_call(
        _outproj_kernel,
        grid=(m // tm,),
        in_specs=[pl.BlockSpec((tm, d), row), pl.BlockSpec((tm, o.shape[1]), row), _resident(w.shape)],
        out_specs=pl.BlockSpec((tm, d), row),
        out_shape=jax.ShapeDtypeStruct((m, d), F32),
        compiler_params=_params("parallel"),
        name="attn_outproj",
    )(x, o, w)


def kernel(x_prompt, x_sample, cache_k, cache_v, page_table, a_norm_g, a_w_in, a_b_in, a_ln_g, a_ln_b,
           a_w_s, a_b_s, a_w_out, m_norm_g, m_w1, m_w2, kv_norm_g, w_k, w_v, b_norm_g, b_w_q, b_lq1,
           b_lk1, b_lq2, b_lk2, b_subln_g, b_w_o, final_norm_g):
    bp, sp, d = x_prompt.shape
    bs, ss, _ = x_sample.shape
    assert bp == 1 and ss == 1, "one prompt sequence and single-token sample rows"
    assert a_norm_g.shape[0] == 1 and b_norm_g.shape[0] == 1 and m_norm_g.shape[0] == 2
    n_kh, dh = cache_k.shape[2:]
    n_vh, vd = cache_v.shape[2:]
    dg = a_ln_g.shape[-1]
    gd = dg // N_GROUPS_A
    assert n_vh == N_HEADS_B and n_kh == 2 * N_HEADS_B and vd == 2 * dh == V7X_LANES and gd == V7X_LANES

    xp = x_prompt.reshape(sp, d)
    xs = x_sample.reshape(bs, d)
    vec = lambda a: a.reshape(1, -1).astype(F32)
    bf = lambda a: a.astype(BF16)

    mix_common = (vec(a_norm_g[0]), bf(a_w_in[0]), vec(a_b_in[0]), vec(a_ln_g[0]), vec(a_ln_b[0]))
    w_out = bf(a_w_out[0])
    per_group = lambda a: jnp.broadcast_to(a[..., None], a.shape + (gd,)).reshape(a.shape[:-1] + (dg,))
    bias_full = per_group(a_b_s[0].T)
    hp = _mixer_prompt(xp, *mix_common, a_w_s[0], bias_full, w_out, tm=256)
    scale_row = per_group(a_w_s[0][:, 0, 0]).reshape(1, dg)
    bias_row = bias_full[0:1]
    hs, chunk_v = _mixer_sample(xs, *mix_common, scale_row, bias_row, w_out)

    w1, w2 = bf(m_w1), bf(m_w2)
    hs = _mlp(hs, vec(m_norm_g[0]), w1, w2, layer=0, tm=bs, tf=2048)
    q_scale = dh ** -0.5 * LOG2_E
    proj = (vec(kv_norm_g), vec(b_norm_g[0]), bf(w_k), bf(w_v), bf(b_w_q[0]))
    ks, vs, _, _, qsb = _kvq(hs, *proj, tm=bs, q_scale=q_scale)

    lam_init = _lambda_init(1)
    f = lambda a: a.astype(F32)
    lam = (jnp.exp(jnp.sum(f(b_lq1[0]) * f(b_lk1[0]))) - jnp.exp(jnp.sum(f(b_lq2[0]) * f(b_lk2[0])))
           + lam_init).reshape(1).astype(F32)
    subln_g = vec(b_subln_g[0])
    out_gain = 1.0 - lam_init

    mlp_tm = sp // (bs // 2)
    mlp_tf = m_w1.shape[2] // (page_table.shape[1] // DECODE_PAGES_PER_STEP)
    decode = lambda first: (page_table, first, lam, qsb, ks, vs, subln_g, cache_k, cache_v, out_gain)

    hp, os_a = _mlp(hp, vec(m_norm_g[0]), w1, w2, None, decode(0), layer=0, tm=mlp_tm, tf=mlp_tf,
                    row_group=MLP_ROW_GROUP)

    kp, vp, kpb, vpb, qpb = _kvq(hp, *proj, tm=256, q_scale=q_scale)
    op = _attn_prompt(qpb, kpb, vpb, lam, subln_g, tq=512, out_gain=out_gain)
    w_o = bf(b_w_o[0])
    hp = _outproj(hp, op, w_o, tm=512)

    yp, os_b = _mlp(hp, vec(m_norm_g[1]), w1, w2, vec(final_norm_g), decode(bs // 2),
                    layer=1, tm=mlp_tm, tf=mlp_tf, row_group=MLP_ROW_GROUP)

    hs = _outproj(hs, jnp.concatenate([os_a, os_b], axis=0), w_o, tm=bs)
    ys = _mlp(hs, vec(m_norm_g[1]), w1, w2, vec(final_norm_g), layer=1, tm=bs, tf=2048)

    return (yp.reshape(bp, sp, d), ys.reshape(bs, ss, d),
            kp.reshape(bp, sp, n_kh, dh), vp.reshape(bp, sp, n_vh, vd),
            ks.reshape(bs, ss, n_kh, dh), vs.reshape(bs, ss, n_vh, vd),
            chunk_v.reshape(1, bs, ss, dg))
```

```python
import functools
import math

import jax
import jax.numpy as jnp
import numpy as np
from jax import lax
from jax.experimental import pallas as pl
from jax.experimental.pallas import tpu as pltpu

F32 = jnp.float32
BF16 = jnp.bfloat16

CHUNK = 128
N_GROUPS_A = 16
N_HEADS_B = 16
RMS_EPS = 1e-5
LN_EPS = 1e-5
SQRT_HALF = math.sqrt(0.5)
LOG2_E = math.log2(math.e)
DECODE_PAGES_PER_STEP = 4
MLP_ROW_GROUP = 2

V7X_LANES = 128
V7X_VMEM_LIMIT_BYTES = 56 * 1024 * 1024


def _lambda_init(layer):
    return 0.8 - 0.6 * math.exp(-0.3 * layer)


def _rms_scale(x):
    return x * lax.rsqrt(jnp.mean(x * x, axis=-1, keepdims=True) + RMS_EPS)


def _gelu_exact(z):
    return 0.5 * z * (1.0 + lax.erf(z * SQRT_HALF))


def _resident(shape):
    return pl.BlockSpec(shape, lambda *_: (0,) * len(shape), pipeline_mode=pl.Buffered(1))


def _params(*semantics):
    return pltpu.CompilerParams(dimension_semantics=semantics,
                                vmem_limit_bytes=V7X_VMEM_LIMIT_BYTES)


def _mixer_front(x, g_ref, win_ref, bin_ref, lng_ref, lnb_ref):
    d_gate = lng_ref.shape[-1]
    h = (_rms_scale(x) * g_ref[...]).astype(BF16)
    z = jnp.dot(h, win_ref[...], preferred_element_type=F32) + bin_ref[...]
    z = _gelu_exact(z)
    u = z[:, :d_gate]
    v = z[:, d_gate:]
    mu = jnp.mean(v, axis=-1, keepdims=True)
    vc = v - mu
    var = jnp.mean(vc * vc, axis=-1, keepdims=True)
    vn = vc * lax.rsqrt(var + LN_EPS) * lng_ref[...] + lnb_ref[...]
    return u, vn


def _mixer_prompt_kernel(x_ref, g_ref, win_ref, bin_ref, lng_ref, lnb_ref, ws_ref, bias_ref,
                         wout_ref, o_ref):
    x = x_ref[...]
    u, vn = _mixer_front(x, g_ref, win_ref, bin_ref, lng_ref, lnb_ref)
    vb = vn.astype(BF16)
    n_chunks = x.shape[0] // CHUNK
    gd = vn.shape[1] // N_GROUPS_A
    t_pos = lax.broadcasted_iota(jnp.int32, (CHUNK, CHUNK), 0)
    s_pos = lax.broadcasted_iota(jnp.int32, (CHUNK, CHUNK), 1)
    causal = s_pos <= t_pos
    blocks = [[None] * N_GROUPS_A for _ in range(n_chunks)]
    for g in range(N_GROUPS_A):
        w_masked = jnp.where(causal, ws_ref[g], 0.0).astype(BF16)
        rhs = jnp.concatenate(
            [vb[c * CHUNK:(c + 1) * CHUNK, g * gd:(g + 1) * gd] for c in range(n_chunks)], axis=1)
        mg = jnp.dot(w_masked, rhs, preferred_element_type=F32)
        for c in range(n_chunks):
            blocks[c][g] = mg[:, c * gd:(c + 1) * gd]
    bias = bias_ref[...]
    mixed = jnp.concatenate(
        [jnp.concatenate(blocks[c], axis=1) + bias for c in range(n_chunks)], axis=0)
    t = (u * mixed).astype(BF16)
    o_ref[...] = x + jnp.dot(t, wout_ref[...], preferred_element_type=F32)


def _mixer_sample_kernel(x_ref, g_ref, win_ref, bin_ref, lng_ref, lnb_ref, scale_ref, bias_ref,
                         wout_ref, o_ref, vn_ref):
    x = x_ref[...]
    u, vn = _mixer_front(x, g_ref, win_ref, bin_ref, lng_ref, lnb_ref)
    vn_ref[...] = vn
    mixed = vn * scale_ref[...] + bias_ref[...]
    t = (u * mixed).astype(BF16)
    o_ref[...] = x + jnp.dot(t, wout_ref[...], preferred_element_type=F32)


def _mixer_prompt(x, norm_g, w_in, b_in, ln_g, ln_b, w_s, bias_full, w_out, *, tm):
    m, d = x.shape
    dg = ln_g.shape[-1]
    row = lambda i: (i, 0)
    return pl.pallas_call(
        _mixer_prompt_kernel,
        grid=(m // tm,),
        in_specs=[pl.BlockSpec((tm, d), row), _resident((1, d)), _resident((d, 2 * dg)),
                  _resident((1, 2 * dg)), _resident((1, dg)), _resident((1, dg)),
                  _resident(w_s.shape), _resident((CHUNK, dg)), _resident((dg, d))],
        out_specs=pl.BlockSpec((tm, d), row),
        out_shape=jax.ShapeDtypeStruct((m, d), F32),
        compiler_params=_params("parallel"),
        name="mixer_prompt",
    )(x, norm_g, w_in, b_in, ln_g, ln_b, w_s, bias_full, w_out)


def _mixer_sample(x, norm_g, w_in, b_in, ln_g, ln_b, scale_row, bias_row, w_out):
    m, d = x.shape
    dg = ln_g.shape[-1]
    return pl.pallas_call(
        _mixer_sample_kernel,
        grid=(1,),
        in_specs=[_resident((m, d)), _resident((1, d)), _resident((d, 2 * dg)),
                  _resident((1, 2 * dg)), _resident((1, dg)), _resident((1, dg)),
                  _resident((1, dg)), _resident((1, dg)), _resident((dg, d))],
        out_specs=[pl.BlockSpec((m, d), lambda i: (0, 0)), pl.BlockSpec((m, dg), lambda i: (0, 0))],
        out_shape=[jax.ShapeDtypeStruct((m, d), F32), jax.ShapeDtypeStruct((m, dg), F32)],
        compiler_params=_params("arbitrary"),
        name="mixer_sample",
    )(x, norm_g, w_in, b_in, ln_g, ln_b, scale_row, bias_row, w_out)


def _mlp_kernel(*refs, final_norm, decode_pages, out_gain):
    refs = list(refs)
    if decode_pages:
        refs.pop(0)
    x_ref, g_ref, w1_ref, w2_ref = refs[:4]
    refs = refs[4:]
    fg_ref = refs.pop(0) if final_norm else None
    j = pl.program_id(1)
    r = pl.program_id(2)
    if decode_pages:
        lam_ref, q_ref, kn_ref, vn_ref, sg_ref = refs[:5]
        k_refs = refs[5:5 + decode_pages]
        v_refs = refs[5 + decode_pages:5 + 2 * decode_pages]
        o_ref, od_ref, hn_all, acc_all, qbd_all, spread_ref, m_all, l_all, dacc_all = refs[5 + 2 * decode_pages:]
        qbd_ref, m_ref, l_ref, dacc_ref = qbd_all.at[r], m_all.at[r], l_all.at[r], dacc_all.at[r]
        n_rows, d = qbd_ref.shape
        n_heads = n_rows // 2
        page = spread_ref.shape[0]
        dh = d // n_rows
    else:
        o_ref, hn_all, acc_all = refs
    hn_ref, acc_ref = hn_all.at[r], acc_all.at[r]

    @pl.when(j == 0)
    def _():
        x = x_ref[...]
        hn_ref[...] = (_rms_scale(x) * g_ref[...]).astype(BF16)
        acc_ref[...] = x
        if decode_pages:
            row = lax.broadcasted_iota(jnp.int32, (n_rows, d), 0)
            c = lax.broadcasted_iota(jnp.int32, (n_rows, d), 1)
            own = (c // dh) == jnp.where(row < n_heads, 2 * row, 2 * (row - n_heads) + 1)
            q_rows = jnp.where(own, q_ref[...].astype(F32), 0.0)
            qbd_ref[...] = q_rows.astype(BF16)
            pos = lax.broadcasted_iota(jnp.int32, spread_ref.shape, 0)
            slot = lax.broadcasted_iota(jnp.int32, spread_ref.shape, 1)
            spread_ref[...] = jnp.where(slot // n_heads == pos, 1.0, 0.0).astype(BF16)
            m_ref[...] = jnp.sum(q_rows * kn_ref[...], axis=-1, keepdims=True)
            l_ref[...] = jnp.ones(l_ref.shape, F32)
            vn = vn_ref[...]
            dacc_ref[...] = jnp.concatenate([vn, vn], axis=0)

    if decode_pages:
        kt = jnp.concatenate([k[...].astype(BF16) for k in k_refs], axis=1)
        s = jnp.dot(qbd_ref[...], kt, preferred_element_type=F32)
    a = jnp.dot(hn_ref[...], w1_ref[...], preferred_element_type=F32)
    if decode_pages:
        m_prev = m_ref[...]
        m_new = jnp.maximum(m_prev, jnp.max(s, axis=-1, keepdims=True))
        alpha = jnp.exp2(m_prev - m_new)
        pr = jnp.exp2(s - m_new)
        l_ref[...] = alpha * l_ref[...] + jnp.sum(pr, axis=-1, keepdims=True)
        m_ref[...] = m_new
        prb = pr.astype(BF16)
        p_rows = jnp.concatenate([prb[:, p * page:(p + 1) * page] for p in range(decode_pages)], axis=0)
        p_slots = jnp.dot(p_rows, spread_ref[...], preferred_element_type=F32)
    a = jnp.maximum(a, 0.0)
    a = (a * a).astype(BF16)
    acc_ref[...] += jnp.dot(a, w2_ref[...], preferred_element_type=F32)
    if decode_pages:
        rr = lax.broadcasted_iota(jnp.int32, p_slots.shape, 0)
        cc = lax.broadcasted_iota(jnp.int32, p_slots.shape, 1)
        p_own = jnp.where((cc % n_heads) == (rr % n_heads), p_slots, 0.0).astype(BF16)
        p_wide = jnp.concatenate([p_own[p * n_rows:(p + 1) * n_rows] for p in range(decode_pages)], axis=1)
        vp = jnp.concatenate([v[...].astype(BF16) for v in v_refs], axis=0)
        dacc_ref[...] = alpha * dacc_ref[...] + jnp.dot(p_wide, vp, preferred_element_type=F32)

    @pl.when(j == pl.num_programs(1) - 1)
    def _():
        y = acc_ref[...]
        if final_norm:
            y = _rms_scale(y) * fg_ref[...]
        o_ref[...] = y
        if decode_pages:
            o = dacc_ref[...] / l_ref[...]
            od_ref[...] = _diff_subln(o[:n_heads], o[n_heads:], lam_ref[0], sg_ref[...], out_gain
                                      ).astype(od_ref.dtype)


def _mlp(x, g, w1, w2, final_g=None, decode=None, *, layer, tm, tf, row_group=1):
    m, d = x.shape
    f = w1.shape[2]
    final_norm = final_g is not None
    n_tiles, n_hidden = m // tm, f // tf
    grid = (n_tiles // row_group, n_hidden, row_group)
    tile = lambda gi, r: gi * row_group + r
    x_idx = lambda gi, j, r, *_: (jnp.where(j == 0, tile(gi, r), tile(gi, row_group - 1)), 0)
    o_idx = lambda gi, j, r, *_: (jnp.where(j == n_hidden - 1, tile(gi, r), tile(gi, 0)), 0)
    in_specs = [pl.BlockSpec((tm, d), x_idx, pipeline_mode=pl.Buffered(1)), _resident((1, d)),
                pl.BlockSpec((None, d, tf), lambda gi, j, r, *_: (layer, 0, j)),
                pl.BlockSpec((None, tf, d), lambda gi, j, r, *_: (layer, j, 0))]
    args = [x, g, w1, w2]
    out_specs = [pl.BlockSpec((tm, d), o_idx)]
    out_shape = [jax.ShapeDtypeStruct((m, d), F32)]
    scratch = [pltpu.VMEM((row_group, tm, d), BF16), pltpu.VMEM((row_group, tm, d), F32)]
    prefetch = []
    pages_per_step = 0
    out_gain = None
    if final_norm:
        in_specs.append(_resident((1, d)))
        args.append(final_g)
    if decode is not None:
        page_table, first_seq, lam, qb, k_new, v_new, subln_g, cache_k, cache_v, out_gain = decode
        n_seq, n_pages = page_table.shape
        n_phys, page, n_kh, dh = cache_k.shape
        n_vh, vd = cache_v.shape[2:]
        pages_per_step = n_pages // n_hidden
        assert first_seq + n_tiles <= n_seq and pages_per_step * n_hidden == n_pages
        ck = jnp.transpose(cache_k, (0, 2, 3, 1)).reshape(n_phys, n_kh * dh, page)
        cv = cache_v.reshape(n_phys, page * n_vh, vd)
        seq = lambda gi, r: first_seq + tile(gi, r)
        row = lambda: pl.BlockSpec((None, 1, d), lambda gi, j, r, pt: (seq(gi, r), 0, 0))

        def page_spec(p, rows, cols):
            return pl.BlockSpec(
                (None, rows, cols),
                lambda gi, j, r, pt: (pt[seq(gi, r) * n_pages + j * pages_per_step + p], 0, 0))

        prefetch = [page_table.reshape(-1)]
        in_specs += ([pl.BlockSpec(memory_space=pltpu.SMEM), row(), row(),
                      pl.BlockSpec((None, n_vh, vd), lambda gi, j, r, pt: (seq(gi, r), 0, 0)),
                      pl.BlockSpec((1, vd), lambda gi, j, r, pt: (0, 0))]
                     + [page_spec(p, n_kh * dh, page) for p in range(pages_per_step)]
                     + [page_spec(p, page * n_vh, vd) for p in range(pages_per_step)])
        args += [lam, qb.reshape(n_seq, 1, d), k_new.reshape(n_seq, 1, d), v_new.reshape(n_seq, n_vh, vd),
                 subln_g] + [ck] * pages_per_step + [cv] * pages_per_step
        out_specs.append(pl.BlockSpec((None, n_vh, vd), lambda gi, j, r, pt: o_idx(gi, j, r) + (0,)))
        out_shape.append(jax.ShapeDtypeStruct((n_tiles, n_vh, vd), BF16))
        scratch += [pltpu.VMEM((row_group, n_kh, d), BF16), pltpu.VMEM((page, page * n_vh), BF16),
                    pltpu.VMEM((row_group, n_kh, 1), F32), pltpu.VMEM((row_group, n_kh, 1), F32),
                    pltpu.VMEM((row_group, n_kh, vd), F32)]
    out = pl.pallas_call(
        functools.partial(_mlp_kernel, final_norm=final_norm, decode_pages=pages_per_step,
                          out_gain=out_gain),
        grid_spec=pltpu.PrefetchScalarGridSpec(
            num_scalar_prefetch=len(prefetch), grid=grid, in_specs=in_specs, out_specs=out_specs,
            scratch_shapes=scratch),
        out_shape=out_shape,
        compiler_params=_params("parallel", "arbitrary", "arbitrary"),
        name=("mlp_final" if final_norm else "mlp") + ("_decode" if decode is not None else ""),
    )(*prefetch, *args)
    if decode is None:
        return out[0]
    return out[0], out[1].reshape(n_tiles, n_vh * vd)


def _kvq_kernel(x_ref, gk_ref, gq_ref, wk_ref, wv_ref, wq_ref, k_ref, v_ref, kb_ref, vb_ref, qb_ref,
                *, q_scale):
    xs = _rms_scale(x_ref[...])
    nk = (xs * gk_ref[...]).astype(BF16)
    nq = (xs * gq_ref[...]).astype(BF16)
    k = jnp.dot(nk, wk_ref[...], preferred_element_type=F32)
    v = jnp.dot(nk, wv_ref[...], preferred_element_type=F32)
    q = jnp.dot(nq, wq_ref[...], preferred_element_type=F32) * q_scale
    k_ref[...] = k
    v_ref[...] = v
    kb_ref[...] = k.astype(BF16)
    vb_ref[...] = v.astype(BF16)
    qb_ref[...] = q.astype(BF16)


def _kvq(x, gk, gq, wk, wv, wq, *, tm, q_scale):
    m, d = x.shape
    row = lambda i: (i, 0)
    blk = lambda: pl.BlockSpec((tm, d), row)
    return pl.pallas_call(
        functools.partial(_kvq_kernel, q_scale=q_scale),
        grid=(m // tm,),
        in_specs=[blk(), _resident((1, d)), _resident((1, d)),
                  _resident(wk.shape), _resident(wv.shape), _resident(wq.shape)],
        out_specs=[blk(), blk(), blk(), blk(), blk()],
        out_shape=[jax.ShapeDtypeStruct((m, d), F32), jax.ShapeDtypeStruct((m, d), F32),
                   jax.ShapeDtypeStruct((m, d), BF16), jax.ShapeDtypeStruct((m, d), BF16),
                   jax.ShapeDtypeStruct((m, d), BF16)],
        compiler_params=_params("parallel"),
        name="kvq_proj",
    )(x, gk, gq, wk, wv, wq)


def _diff_subln(o1, o2, lam, g, out_gain):
    o = o1 - lam * o2
    return _rms_scale(o) * g * out_gain


ONES_ROWS = 16


def _attn_prompt_kernel(lam_ref, q_ref, qn_ref, k_ref, v_ref, g_ref, o_ref, vt_ref, qt_ref, s0_ref, s1_ref,
                        m_ref, acc_ref, *, out_gain):
    qi = pl.program_id(1)
    tq, dh2 = q_ref.shape
    tk = tq
    dh = dh2 // 2
    seq = k_ref.shape[0]
    t_chunk = tq
    s_refs = (s0_ref, s1_ref)

    def scores(c, slot):
        k = k_ref[pl.ds(pl.multiple_of(c * tk, tk), tk), :]
        s_refs[slot][...] = jnp.dot(k, qt_ref[...], preferred_element_type=F32)

    def start_tile(tile_ref):
        q = tile_ref[...].astype(F32)
        lane = lax.broadcasted_iota(jnp.int32, (1, dh2), 1)
        q_both = jnp.concatenate([jnp.where(lane < dh, q, 0.0), jnp.where(lane >= dh, q, 0.0)], axis=0)
        qt_ref[...] = q_both.T.astype(BF16)
        scores(0, 0)

    @pl.when(qi == 0)
    def _():
        for c in range(seq // t_chunk):
            cols = slice(c * t_chunk, (c + 1) * t_chunk)
            vt_ref[0:dh2, cols] = v_ref[cols, :].astype(F32).T.astype(BF16)
        vt_ref[dh2:, :] = jnp.ones((ONES_ROWS, seq), BF16)
        start_tile(q_ref)

    m_ref[...] = jnp.full(m_ref.shape, -jnp.inf, F32)
    acc_ref[...] = jnp.zeros(acc_ref.shape, F32)

    def consume(c, slot, masked):
        s = s_refs[slot][...]
        if masked:
            k_pos = c * tk + lax.broadcasted_iota(jnp.int32, (tk, 2 * tq), 0)
            col = lax.broadcasted_iota(jnp.int32, (tk, 2 * tq), 1)
            q_pos = qi * tq + jnp.where(col < tq, col, col - tq)
            s = jnp.where(k_pos <= q_pos, s, -jnp.inf)
        m_prev = m_ref[...]
        m_new = jnp.maximum(m_prev, jnp.max(s, axis=0, keepdims=True))
        alpha = jnp.exp2(m_prev - m_new)
        p = jnp.exp2(s - m_new).astype(BF16)
        vt = vt_ref[:, pl.ds(pl.multiple_of(c * tk, tk), tk)]
        acc_ref[...] = alpha * acc_ref[...] + jnp.dot(vt, p, preferred_element_type=F32)
        m_ref[...] = m_new

    def pair(c, carry):
        scores(2 * c + 1, 1)
        consume(2 * c, 0, False)
        scores(2 * c + 2, 0)
        consume(2 * c + 1, 1, False)
        return carry

    def two_pairs(c, carry):
        return pair(2 * c + 1, pair(2 * c, carry))

    lax.fori_loop(0, qi // 4, two_pairs, 0)
    lax.fori_loop(2 * (qi // 4), qi // 2, pair, 0)

    def finish():
        start_tile(qn_ref)
        acc = acc_ref[...]
        o_t = acc[:dh2] / acc[dh2:dh2 + 1]
        d_t = o_t[:, :tq] - lam_ref[0] * o_t[:, tq:]
        o_ref[...] = (_rms_scale(d_t.T) * g_ref[...] * out_gain).astype(o_ref.dtype)

    @pl.when(qi % 2 == 0)
    def _():
        consume(qi, 0, True)
        finish()

    @pl.when(qi % 2 == 1)
    def _():
        scores(qi, 1)
        consume(qi - 1, 0, False)
        consume(qi, 1, True)
        finish()


def _attn_prompt(qb, kb, vb, lam, subln_g, *, tq, out_gain):
    s, d = qb.shape
    dh2 = d // N_HEADS_B
    return pl.pallas_call(
        functools.partial(_attn_prompt_kernel, out_gain=out_gain),
        grid=(N_HEADS_B, s // tq),
        in_specs=[pl.BlockSpec(memory_space=pltpu.SMEM),
                  pl.BlockSpec((tq, dh2), lambda h, i: (i, h)),
                  pl.BlockSpec((tq, dh2), lambda h, i: (jnp.minimum(i + 1, s // tq - 1), h)),
                  pl.BlockSpec((s, dh2), lambda h, i: (0, h)),
                  pl.BlockSpec((s, dh2), lambda h, i: (0, h)),
                  pl.BlockSpec((1, dh2), lambda h, i: (0, 0))],
        out_specs=pl.BlockSpec((tq, dh2), lambda h, i: (i, h)),
        out_shape=jax.ShapeDtypeStruct((s, d), BF16),
        scratch_shapes=[pltpu.VMEM((dh2 + ONES_ROWS, s), BF16), pltpu.VMEM((dh2, 2 * tq), BF16),
                        pltpu.VMEM((tq, 2 * tq), F32), pltpu.VMEM((tq, 2 * tq), F32),
                        pltpu.VMEM((1, 2 * tq), F32), pltpu.VMEM((dh2 + ONES_ROWS, 2 * tq), F32)],
        compiler_params=_params("parallel", "arbitrary"),
        name="attn_prompt",
    )(lam, qb, qb, kb, vb, subln_g)


def _outproj_kernel(x_ref, o_ref, w_ref, y_ref):
    y_ref[...] = x_ref[...] + jnp.dot(o_ref[...], w_ref[...], preferred_element_type=F32)


def _outproj(x, o, w, *, tm):
    m, d = x.shape
    row = lambda i: (i, 0)
    return pl.pallas_call(
        _outproj_kernel,
        grid=(m // tm,),
        in_specs=[pl.BlockSpec((tm, d), row), pl.BlockSpec((tm, o.shape[1]), row), _resident(w.shape)],
        out_specs=pl.BlockSpec((tm, d), row),
        out_shape=jax.ShapeDtypeStruct((m, d), F32),
        compiler_params=_params("parallel"),
        name="attn_outproj",
    )(x, o, w)


def kernel(x_prompt, x_sample, cache_k, cache_v, page_table, a_norm_g, a_w_in, a_b_in, a_ln_g, a_ln_b,
           a_w_s, a_b_s, a_w_out, m_norm_g, m_w1, m_w2, kv_norm_g, w_k, w_v, b_norm_g, b_w_q, b_lq1,
           b_lk1, b_lq2, b_lk2, b_subln_g, b_w_o, final_norm_g):
    bp, sp, d = x_prompt.shape
    bs, ss, _ = x_sample.shape
    assert bp == 1 and ss == 1, "one prompt sequence and single-token sample rows"
    assert a_norm_g.shape[0] == 1 and b_norm_g.shape[0] == 1 and m_norm_g.shape[0] == 2
    n_kh, dh = cache_k.shape[2:]
    n_vh, vd = cache_v.shape[2:]
    dg = a_ln_g.shape[-1]
    gd = dg // N_GROUPS_A
    assert n_vh == N_HEADS_B and n_kh == 2 * N_HEADS_B and vd == 2 * dh == V7X_LANES and gd == V7X_LANES

    xp = x_prompt.reshape(sp, d)
    xs = x_sample.reshape(bs, d)
    vec = lambda a: a.reshape(1, -1).astype(F32)
    bf = lambda a: a.astype(BF16)

    mix_common = (vec(a_norm_g[0]), bf(a_w_in[0]), vec(a_b_in[0]), vec(a_ln_g[0]), vec(a_ln_b[0]))
    w_out = bf(a_w_out[0])
    per_group = lambda a: jnp.broadcast_to(a[..., None], a.shape + (gd,)).reshape(a.shape[:-1] + (dg,))
    bias_full = per_group(a_b_s[0].T)
    hp = _mixer_prompt(xp, *mix_common, a_w_s[0], bias_full, w_out, tm=256)
    scale_row = per_group(a_w_s[0][:, 0, 0]).reshape(1, dg)
    bias_row = bias_full[0:1]
    hs, chunk_v = _mixer_sample(xs, *mix_common, scale_row, bias_row, w_out)

    w1, w2 = bf(m_w1), bf(m_w2)
    hs = _mlp(hs, vec(m_norm_g[0]), w1, w2, layer=0, tm=bs, tf=2048)
    q_scale = dh ** -0.5 * LOG2_E
    proj = (vec(kv_norm_g), vec(b_norm_g[0]), bf(w_k), bf(w_v), bf(b_w_q[0]))
    ks, vs, _, _, qsb = _kvq(hs, *proj, tm=bs, q_scale=q_scale)

    lam_init = _lambda_init(1)
    f = lambda a: a.astype(F32)
    lam = (jnp.exp(jnp.sum(f(b_lq1[0]) * f(b_lk1[0]))) - jnp.exp(jnp.sum(f(b_lq2[0]) * f(b_lk2[0])))
           + lam_init).reshape(1).astype(F32)
    subln_g = vec(b_subln_g[0])
    out_gain = 1.0 - lam_init

    mlp_tm = sp // (bs // 2)
    mlp_tf = m_w1.shape[2] // (page_table.shape[1] // DECODE_PAGES_PER_STEP)
    decode = lambda first: (page_table, first, lam, qsb, ks, vs, subln_g, cache_k, cache_v, out_gain)

    hp, os_a = _mlp(hp, vec(m_norm_g[0]), w1, w2, None, decode(0), layer=0, tm=mlp_tm, tf=mlp_tf,
                    row_group=MLP_ROW_GROUP)

    kp, vp, kpb, vpb, qpb = _kvq(hp, *proj, tm=256, q_scale=q_scale)
    op = _attn_prompt(qpb, kpb, vpb, lam, subln_g, tq=512, out_gain=out_gain)
    w_o = bf(b_w_o[0])
    hp = _outproj(hp, op, w_o, tm=512)

    yp, os_b = _mlp(hp, vec(m_norm_g[1]), w1, w2, vec(final_norm_g), decode(bs // 2),
                    layer=1, tm=mlp_tm, tf=mlp_tf, row_group=MLP_ROW_GROUP)

    hs = _outproj(hs, jnp.concatenate([os_a, os_b], axis=0), w_o, tm=bs)
    ys = _mlp(hs, vec(m_norm_g[1]), w1, w2, vec(final_norm_g), layer=1, tm=bs, tf=2048)

    return (yp.reshape(bp, sp, d), ys.reshape(bs, ss, d),
            kp.reshape(bp, sp, n_kh, dh), vp.reshape(bp, sp, n_vh, vd),
            ks.reshape(bs, ss, n_kh, dh), vs.reshape(bs, ss, n_vh, vd),
            chunk_v.reshape(1, bs, ss, dg))
```
